```python
import math
import jax, jax.numpy as jnp
from jax import lax
import numpy as np

D_MODEL = 1024
BATCH = 16
SEQ = 2048
DEPTH = 2

CHUNK = 64
Q_BLOCK = 128
N_MIXERS = 4
GROUP_WIDTH = D_MODEL // N_MIXERS
HEAD_DIM = 64
HEADS_PER_GROUP = GROUP_WIDTH // HEAD_DIM
CONV_WIDTH = 3
SGU_BLOCK = 128
PROJ_WIDTH = 11 * GROUP_WIDTH + HEADS_PER_GROUP
N_EXPERT_GROUPS = 4
EXPERTS_PER_GROUP = 8
N_EXPERTS = N_EXPERT_GROUPS * EXPERTS_PER_GROUP
TOP_K = 2
D_EXPERT = D_MODEL // 4
ROW_BLOCK = 256
ALPHA = (2.0 * DEPTH) ** 0.25
BETA = (8.0 * DEPTH) ** -0.25
LN_EPS = 1e-5
RMS_EPS = 1e-6

kernel_name = "chunk_causal_hybrid_headgroup_moe_encoder"


def layer_norm(x, g, b):
    xf = x.astype(jnp.float32)
    mu = jnp.mean(xf, axis=-1, keepdims=True)
    var = jnp.mean(jnp.square(xf - mu), axis=-1, keepdims=True)
    y = (xf - mu) * lax.rsqrt(var + LN_EPS) * g.astype(jnp.float32) + b.astype(jnp.float32)
    return y.astype(x.dtype)


def group_rms_norm(x, g):
    B, S, D = x.shape
    xf = x.astype(jnp.float32).reshape(B, S, N_MIXERS, GROUP_WIDTH)
    xf = xf * lax.rsqrt(jnp.mean(jnp.square(xf), axis=-1, keepdims=True) + RMS_EPS)
    return (xf.reshape(B, S, D) * g.astype(jnp.float32)).astype(x.dtype)


def to_heads(t):
    B, S, _ = t.shape
    return t.reshape(B, S, HEADS_PER_GROUP, HEAD_DIM).transpose(0, 2, 1, 3)


def from_heads(t):
    B, H, S, d = t.shape
    return t.transpose(0, 2, 1, 3).reshape(B, S, H * d)


def stick_breaking_attention(q, k, v):
    B, H, S, d = q.shape
    nq = S // Q_BLOCK
    qb = q.reshape(B, H, nq, Q_BLOCK, d).transpose(2, 0, 1, 3, 4)
    k_pos = jnp.arange(S)
    scale = 1.0 / math.sqrt(d)

    def block(args):
        q_blk, i = args
        z = jnp.einsum('bhqd,bhkd->bhqk', q_blk, k).astype(jnp.float32) * scale
        q_pos = i * Q_BLOCK + jnp.arange(Q_BLOCK)
        mask = k_pos[None, :] < q_pos[:, None]
        log_1m = jnp.where(mask, jax.nn.log_sigmoid(-z), 0.0)
        later = lax.cumsum(log_1m, axis=3, reverse=True) - log_1m
        w = jnp.where(mask, jnp.exp(jax.nn.log_sigmoid(z) + later), 0.0)
        return jnp.einsum('bhqk,bhkd->bhqd', w.astype(v.dtype), v)

    out = lax.map(block, (qb, jnp.arange(nq)))
    return out.transpose(1, 2, 0, 3, 4).reshape(B, H, S, d)


def forgetting_attention(q, k, v, log_f):
    B, H, S, d = q.shape
    nq = S // Q_BLOCK
    c = lax.cumsum(log_f, axis=2)
    qb = q.reshape(B, H, nq, Q_BLOCK, d).transpose(2, 0, 1, 3, 4)
    cb = c.reshape(B, H, nq, Q_BLOCK).transpose(2, 0, 1, 3)
    k_pos = jnp.arange(S)
    scale = 1.0 / math.sqrt(d)
    neg = jnp.finfo(jnp.float32).min

    def block(args):
        q_blk, c_blk, i = args
        s = jnp.einsum('bhqd,bhkd->bhqk', q_blk, k).astype(jnp.float32) * scale
        s = s + c_blk[..., :, None] - c[:, :, None, :]
        q_pos = i * Q_BLOCK + jnp.arange(Q_BLOCK)
        mask = k_pos[None, :] <= q_pos[:, None]
        p = jax.nn.softmax(jnp.where(mask, s, neg), axis=-1)
        return jnp.einsum('bhqk,bhkd->bhqd', p.astype(v.dtype), v)

    out = lax.map(block, (qb, cb, jnp.arange(nq)))
    return out.transpose(1, 2, 0, 3, 4).reshape(B, H, S, d)


def short_gated_conv(h, b_gate, c_gate, conv_w, conv_b):
    S = h.shape[1]
    z = c_gate * h
    zp = jnp.pad(z, ((0, 0), (CONV_WIDTH - 1, 0), (0, 0)))
    y = conv_b + conv_w[0] * zp[:, 0:S]
    for tap in range(1, CONV_WIDTH):
        y = y + conv_w[tap] * zp[:, tap:tap + S]
    return b_gate * y


def spatial_gating(u, v, ln_g, ln_b, w_s, b_s):
    v = layer_norm(v, ln_g, ln_b)
    B, S, W = v.shape
    nb = S // SGU_BLOCK
    vb = v.reshape(B, nb, SGU_BLOCK, HEADS_PER_GROUP, HEAD_DIM)
    chunk_id = jnp.arange(SGU_BLOCK) // CHUNK
    mask = chunk_id[None, :] <= chunk_id[:, None]
    w_m = jnp.where(mask[None], w_s, 0.0).astype(v.dtype)
    mixed = jnp.einsum('gij,bnjgc->bnigc', w_m, vb) + b_s.T[None, None, :, :, None]
    return u * mixed.reshape(B, S, W)


def mixer_sublayer(x, w_in, b_f, conv_w, conv_b, sgu_ln_g, sgu_ln_b, sgu_w, sgu_b, grp_g, w_out):
    W = GROUP_WIDTH
    proj = x @ w_in
    qkv_a, qkv_b, f_b, conv_in, sgu_in = jnp.split(
        proj, [3 * W, 6 * W, 6 * W + HEADS_PER_GROUP, 9 * W + HEADS_PER_GROUP], axis=-1)
    qa, ka, va = jnp.split(qkv_a, 3, axis=-1)
    out_a = from_heads(stick_breaking_attention(to_heads(qa), to_heads(ka), to_heads(va)))
    qb, kb, vb = jnp.split(qkv_b, 3, axis=-1)
    log_f = jax.nn.log_sigmoid(f_b.astype(jnp.float32) + b_f.astype(jnp.float32)).transpose(0, 2, 1)
    out_b = from_heads(forgetting_attention(to_heads(qb), to_heads(kb), to_heads(vb), log_f))
    h_c, b_c, c_c = jnp.split(conv_in, 3, axis=-1)
    out_c = short_gated_conv(h_c, b_c, c_c, conv_w, conv_b)
    u_d, v_d = jnp.split(jax.nn.gelu(sgu_in), 2, axis=-1)
    out_d = spatial_gating(u_d, v_d, sgu_ln_g, sgu_ln_b, sgu_w, sgu_b)
    mix = group_rms_norm(jnp.concatenate([out_a, out_b, out_c, out_d], axis=-1), grp_g)
    return mix @ w_out


def hierarchical_moe(x, router_g_w, router_g_b, router_e_w, router_e_b, w1, w3, w2):
    Bsz, S, D = x.shape
    xf = x.reshape(-1, D)
    T = xf.shape[0]
    p_g = jax.nn.softmax((xf @ router_g_w).astype(jnp.float32) + router_g_b.astype(jnp.float32), axis=-1)
    g_sel = jnp.argmax(p_g, axis=-1)
    p_gsel = jnp.take_along_axis(p_g, g_sel[:, None], axis=-1)
    e_logits = ((xf @ router_e_w).astype(jnp.float32) + router_e_b.astype(jnp.float32)).reshape(
        T, N_EXPERT_GROUPS, EXPERTS_PER_GROUP)
    e_in = jnp.take_along_axis(e_logits, g_sel[:, None, None], axis=1)[:, 0]
    top_p, top_i = lax.top_k(jax.nn.softmax(e_in, axis=-1), TOP_K)
    gate = p_gsel * top_p / jnp.sum(top_p, axis=-1, keepdims=True)
    expert = g_sel[:, None] * EXPERTS_PER_GROUP + top_i

    A = T * TOP_K
    e_flat = expert.reshape(-1)
    tok_flat = jnp.repeat(jnp.arange(T, dtype=jnp.int32), TOP_K)
    w_flat = gate.reshape(-1)
    order = jnp.argsort(e_flat)
    e_s, tok_s, w_s = e_flat[order], tok_flat[order], w_flat[order]
    counts = jax.ops.segment_sum(jnp.ones((A,), jnp.int32), e_flat, num_segments=N_EXPERTS)
    starts = jnp.cumsum(counts) - counts
    padded = (counts + ROW_BLOCK - 1) // ROW_BLOCK * ROW_BLOCK
    p_ends = jnp.cumsum(padded)
    p_starts = p_ends - padded
    dest = p_starts[e_s] + jnp.arange(A, dtype=jnp.int32) - starts[e_s]
    P = -(-A // ROW_BLOCK) * ROW_BLOCK + N_EXPERTS * ROW_BLOCK
    n_blocks = P // ROW_BLOCK
    tok_pad = jnp.zeros((P,), jnp.int32).at[dest].set(tok_s)
    w_pad = jnp.zeros((P,), jnp.float32).at[dest].set(w_s)
    block_expert = jnp.clip(jnp.searchsorted(p_ends, jnp.arange(n_blocks) * ROW_BLOCK, side='right'),
                            0, N_EXPERTS - 1)
    xs = xf[tok_pad].reshape(n_blocks, ROW_BLOCK, D)

    def expert_block(args):
        xb, e = args
        h = jax.nn.silu(xb @ w1[e]) * (xb @ w3[e])
        return h @ w2[e]

    ys = lax.map(expert_block, (xs, block_expert)).reshape(P, D)
    y = jax.ops.segment_sum(ys * w_pad[:, None].astype(ys.dtype), tok_pad, num_segments=T)
    return y.reshape(Bsz, S, D)


def setup_inputs(seed: int = 0) -> dict:
    key = jax.random.key(seed)
    ks = jax.random.split(key, 24)
    f32 = jnp.float32
    n = lambda k, shape: jax.random.normal(k, shape, f32)
    D, W, G, L = D_MODEL, GROUP_WIDTH, HEADS_PER_GROUP, DEPTH
    return {
        "x": n(ks[0], (BATCH, SEQ, D)),
        "ln_in_g": 1.0 + 0.02 * n(ks[1], (D,)),
        "ln_in_b": 0.02 * n(ks[2], (D,)),
        "w_in": n(ks[3], (L, D, PROJ_WIDTH)) * D ** -0.5,
        "b_f": 2.0 + 0.5 * n(ks[4], (L, G)),
        "conv_w": 0.5 * n(ks[5], (L, CONV_WIDTH, W)),
        "conv_b": 0.02 * n(ks[6], (L, W)),
        "sgu_ln_g": 1.0 + 0.02 * n(ks[7], (L, W)),
        "sgu_ln_b": 0.02 * n(ks[8], (L, W)),
        "sgu_w": n(ks[9], (L, G, SGU_BLOCK, SGU_BLOCK)) * SGU_BLOCK ** -0.5,
        "sgu_b": 1.0 + 0.02 * n(ks[10], (L, G, SGU_BLOCK)),
        "grp_g": 1.0 + 0.02 * n(ks[11], (L, D)),
        "w_out": n(ks[12], (L, D, D)) * D ** -0.5 * BETA,
        "ln1_g": 1.0 + 0.02 * n(ks[13], (L, D)),
        "ln1_b": 0.02 * n(ks[14], (L, D)),
        "router_g_w": n(ks[15], (L, D, N_EXPERT_GROUPS)) * D ** -0.5,
        "router_g_b": 0.01 * n(ks[16], (L, N_EXPERT_GROUPS)),
        "router_e_w": n(ks[17], (L, D, N_EXPERTS)) * D ** -0.5,
        "router_e_b": 0.01 * n(ks[18], (L, N_EXPERTS)),
        "w1": n(ks[19], (L, N_EXPERTS, D, D_EXPERT)) * D ** -0.5,
        "w3": n(ks[20], (L, N_EXPERTS, D, D_EXPERT)) * D ** -0.5,
        "w2": n(ks[21], (L, N_EXPERTS, D_EXPERT, D)) * D_EXPERT ** -0.5 * BETA,
        "ln2_g": 1.0 + 0.02 * n(ks[22], (L, D)),
        "ln2_b": 0.02 * n(ks[23], (L, D)),
    }


def reference(x, ln_in_g, ln_in_b, w_in, b_f, conv_w, conv_b, sgu_ln_g, sgu_ln_b, sgu_w, sgu_b,
              grp_g, w_out, ln1_g, ln1_b, router_g_w, router_g_b, router_e_w, router_e_b,
              w1, w3, w2, ln2_g, ln2_b):
    x = layer_norm(x, ln_in_g, ln_in_b)
    for i in range(DEPTH):
        mixed = mixer_sublayer(x, w_in[i], b_f[i], conv_w[i], conv_b[i], sgu_ln_g[i], sgu_ln_b[i],
                               sgu_w[i], sgu_b[i], grp_g[i], w_out[i])
        x = layer_norm(ALPHA * x + mixed, ln1_g[i], ln1_b[i])
        ffn = hierarchical_moe(x, router_g_w[i], router_g_b[i], router_e_w[i], router_e_b[i],
                               w1[i], w3[i], w2[i])
        x = layer_norm(ALPHA * x + ffn, ln2_g[i], ln2_b[i])
    return x
```

```python
import functools
import math

import jax
import jax.numpy as jnp
from jax import lax
from jax.experimental import pallas as pl
from jax.experimental.pallas import tpu as pltpu

N_MIXERS = 4
HEAD_DIM = 64
HEADS = 4
CONV_WIDTH = 3
SGU_BLOCK = 128
CHUNK = 64
N_EXPERT_GROUPS = 4
EXPERTS_PER_GROUP = 8
N_EXPERTS = N_EXPERT_GROUPS * EXPERTS_PER_GROUP
TOP_K = 2
LN_EPS = 1e-5
RMS_EPS = 1e-6

LANES = 128
V7X_VMEM_BYTES = 64 * 1024 * 1024
VMEM_LIMIT = 48 * 1024 * 1024

ROW_TILE = 512
ATT_Q = 256
ATT_K = 256
EXPERT_ROWS = 256
ROUTE_LANES = LANES

F32 = jnp.float32
BF16 = jnp.bfloat16


def _cparams(sem, vmem=None):
    return pltpu.CompilerParams(dimension_semantics=sem, vmem_limit_bytes=vmem)


def _layer_norm_rows(y, g, b):
    mu = jnp.mean(y, axis=-1, keepdims=True)
    d = y - mu
    var = jnp.mean(d * d, axis=-1, keepdims=True)
    return d * lax.rsqrt(var + LN_EPS) * g + b


def _softplus(z):
    return jnp.maximum(z, 0.0) + jnp.log1p(jnp.exp(-jnp.abs(z)))


def _ln_kernel(x_ref, g_ref, b_ref, o_ref):
    o_ref[...] = _layer_norm_rows(x_ref[...], g_ref[...], b_ref[...])


def _layer_norm_call(x, g, b):
    T, D = x.shape
    return pl.pallas_call(
        _ln_kernel,
        out_shape=jax.ShapeDtypeStruct((T, D), F32),
        grid=(T // ROW_TILE,),
        in_specs=[pl.BlockSpec((ROW_TILE, D), lambda i: (i, 0)),
                  pl.BlockSpec((1, D), lambda i: (0, 0)),
                  pl.BlockSpec((1, D), lambda i: (0, 0))],
        out_specs=pl.BlockSpec((ROW_TILE, D), lambda i: (i, 0)),
        compiler_params=_cparams(("parallel",)),
        name="ln_in",
    )(x, g.reshape(1, D), b.reshape(1, D))


def _inproj_kernel(x_ref, w_ref, qa_ref, qb_ref, cv_ref, sg_ref, f_ref, *, widths):
    xb = x_ref[...].astype(BF16)
    off = 0
    for ref, wd in zip((qa_ref, qb_ref, cv_ref, sg_ref, f_ref), widths):
        ref[...] = jnp.dot(xb, w_ref[:, off:off + wd], preferred_element_type=F32).astype(ref.dtype)
        off += wd


def _inproj_call(x, w_all, W):
    T, D = x.shape
    widths = (3 * W, 3 * W, 3 * W, 2 * W, LANES)
    NC = sum(widths)
    dts = (BF16, BF16, F32, F32, F32)
    return pl.pallas_call(
        functools.partial(_inproj_kernel, widths=widths),
        out_shape=[jax.ShapeDtypeStruct((T, wd), dt) for wd, dt in zip(widths, dts)],
        grid=(T // ROW_TILE,),
        in_specs=[pl.BlockSpec((ROW_TILE, D), lambda i: (i, 0)),
                  pl.BlockSpec((D, NC), lambda i: (0, 0))],
        out_specs=[pl.BlockSpec((ROW_TILE, wd), lambda i: (i, 0)) for wd in widths],
        compiler_params=_cparams(("parallel",), VMEM_LIMIT),
        name="inproj",
    )(x, w_all)


def _fcum_kernel(f_ref, b_ref, tri_ref, c_ref):
    R, S = f_ref.shape
    y = f_ref[...] + b_ref[...]
    lf = -_softplus(-y)
    tri = tri_ref[...]
    carry = jnp.zeros((R, 1), F32)
    for blk in range(S // LANES):
        seg = lf[:, blk * LANES:(blk + 1) * LANES]
        s1 = seg.astype(BF16)
        r1 = seg - s1.astype(F32)
        s2 = r1.astype(BF16)
        s3 = (r1 - s2.astype(F32)).astype(BF16)
        cs = (jnp.dot(s1, tri, preferred_element_type=F32)
              + jnp.dot(s2, tri, preferred_element_type=F32)
              + jnp.dot(s3, tri, preferred_element_type=F32)) + carry
        c_ref[:, blk * LANES:(blk + 1) * LANES] = cs
        carry = cs[:, LANES - 1:LANES]


def _fcum_call(f_rows, b_rows):
    R, S = f_rows.shape
    idx = jnp.arange(LANES)
    tri = (idx[:, None] <= idx[None, :]).astype(BF16)
    return pl.pallas_call(
        _fcum_kernel,
        out_shape=jax.ShapeDtypeStruct((R, S), F32),
        name="forget_cumsum",
    )(f_rows, b_rows, tri)


def _rms_group_store(o_ref, out, g_ref):
    ms = jnp.mean(out * out, axis=-1, keepdims=True)
    o_ref[...] = (out * lax.rsqrt(ms + RMS_EPS) * g_ref[...]).astype(o_ref.dtype)


def _head_masks():
    lane = lax.broadcasted_iota(jnp.int32, (1, LANES), 1)
    return [lane < HEAD_DIM, lane >= HEAD_DIM]


def _nt_dot(a, b):
    return lax.dot_general(a, b, (((1,), (1,)), ((), ())), preferred_element_type=F32)


def _sb_attn_kernel(q_ref, k_ref, v_ref, m_ref, g_ref, o_ref):
    i = pl.program_id(1)
    Q = q_ref.shape[0]
    row = lax.broadcasted_iota(jnp.int32, (Q, ATT_K), 0)
    col = lax.broadcasted_iota(jnp.int32, (Q, ATT_K), 1)
    diag_mask = col < row
    later_mat = m_ref[...]
    hmasks = _head_masks()

    def tile(qm, j, pair, carry, acc, mask):
        ks = pl.multiple_of(j * ATT_K, ATT_K)
        k_t = k_ref[pl.ds(ks, ATT_K), pair * LANES:(pair + 1) * LANES]
        v_t = v_ref[pl.ds(ks, ATT_K), pair * LANES:(pair + 1) * LANES]
        z = _nt_dot(qm, k_t)
        sp = _softplus(z)
        log_1m = -sp
        if mask is not None:
            log_1m = jnp.where(mask, log_1m, 0.0)
        hi = log_1m.astype(BF16)
        lo = (log_1m - hi.astype(F32)).astype(BF16)
        later = (jnp.dot(hi, later_mat, preferred_element_type=F32)
                 + jnp.dot(lo, later_mat, preferred_element_type=F32)) + carry
        w = jnp.exp((z - sp) + later)
        if mask is not None:
            w = jnp.where(mask, w, 0.0)
        acc = acc + jnp.dot(w.astype(BF16), v_t, preferred_element_type=F32)
        carry = later[:, 0:1] + log_1m[:, 0:1]
        return carry, acc

    outs = []
    for pair in range(HEADS // 2):
        q_pair = q_ref[:, pair * LANES:(pair + 1) * LANES]
        accs = []
        for hh in range(2):
            qm = jnp.where(hmasks[hh], q_pair, jnp.zeros_like(q_pair))
            carry0 = jnp.zeros((Q, 1), F32)
            acc0 = jnp.zeros((Q, LANES), F32)
            carry, acc = tile(qm, i, pair, carry0, acc0, diag_mask)

            def body(jj, c, qm=qm, pair=pair):
                return tile(qm, i - jj, pair, c[0], c[1], None)

            carry, acc = lax.fori_loop(1, i + 1, body, (carry, acc))
            accs.append(acc)
        outs.append(jnp.where(hmasks[0], accs[0], accs[1]))
    _rms_group_store(o_ref, jnp.concatenate(outs, axis=-1), g_ref)


def _fox_attn_kernel(q_ref, k_ref, v_ref, c_ref, g_ref, o_ref):
    i = pl.program_id(1)
    Q = q_ref.shape[0]
    nk = c_ref.shape[1] // HEADS
    row = lax.broadcasted_iota(jnp.int32, (Q, ATT_K), 0)
    col = lax.broadcasted_iota(jnp.int32, (Q, ATT_K), 1)
    diag_mask = col <= row
    hmasks = _head_masks()
    neg = jnp.finfo(F32).min

    def tile(qm, j, pair, h, m, l, acc, mask):
        ks = pl.multiple_of(j * ATT_K, ATT_K)
        k_t = k_ref[pl.ds(ks, ATT_K), pair * LANES:(pair + 1) * LANES]
        v_t = v_ref[pl.ds(ks, ATT_K), pair * LANES:(pair + 1) * LANES]
        c_row = c_ref[0, pl.ds(h * nk + j, 1), :]
        s = _nt_dot(qm, k_t) - c_row
        if mask is not None:
            s = jnp.where(mask, s, neg)
        m_new = jnp.maximum(m, jnp.max(s, axis=-1, keepdims=True))
        alpha = jnp.exp(m - m_new)
        p = jnp.exp(s - m_new)
        l = alpha * l + jnp.sum(p, axis=-1, keepdims=True)
        acc = alpha * acc + jnp.dot(p.astype(BF16), v_t, preferred_element_type=F32)
        return m_new, l, acc

    outs = []
    for pair in range(HEADS // 2):
        q_pair = q_ref[:, pair * LANES:(pair + 1) * LANES]
        accs = []
        for hh in range(2):
            h = 2 * pair + hh
            qm = jnp.where(hmasks[hh], q_pair, jnp.zeros_like(q_pair))
            m0 = jnp.full((Q, 1), neg, F32)
            l0 = jnp.zeros((Q, 1), F32)
            acc0 = jnp.zeros((Q, LANES), F32)
            m, l, acc = tile(qm, i, pair, h, m0, l0, acc0, diag_mask)

            def body(jj, c, qm=qm, pair=pair, h=h):
                return tile(qm, i - jj, pair, h, c[0], c[1], c[2], None)

            m, l, acc = lax.fori_loop(1, i + 1, body, (m, l, acc))
            accs.append(acc / l)
        outs.append(jnp.where(hmasks[0], accs[0], accs[1]))
    _rms_group_store(o_ref, jnp.concatenate(outs, axis=-1), g_ref)


def _attn_specs(B, S, W):
    nq = S // ATT_Q
    q_spec = pl.BlockSpec((ATT_Q, W), lambda b, i: (b * nq + i, 0))
    k_spec = pl.BlockSpec((S, W), lambda b, i: (b, 1))
    v_spec = pl.BlockSpec((S, W), lambda b, i: (b, 2))
    g_spec = pl.BlockSpec((1, W), lambda b, i: (0, 0))
    o_spec = pl.BlockSpec((ATT_Q, W), lambda b, i: (b * nq + i, 0))
    return nq, q_spec, k_spec, v_spec, g_spec, o_spec


def _sb_attn_call(qkv, gain, B, S, W):
    T = B * S
    nq, q_spec, k_spec, v_spec, g_spec, o_spec = _attn_specs(B, S, W)
    idx = jnp.arange(ATT_K)
    later_mat = (idx[:, None] > idx[None, :]).astype(BF16)
    return pl.pallas_call(
        _sb_attn_kernel,
        out_shape=jax.ShapeDtypeStruct((T, W), BF16),
        grid=(B, nq),
        in_specs=[q_spec, k_spec, v_spec,
                  pl.BlockSpec((ATT_K, ATT_K), lambda b, i: (0, 0)), g_spec],
        out_specs=o_spec,
        compiler_params=_cparams(("parallel", "arbitrary"), VMEM_LIMIT),
        name="sb_attn",
    )(qkv, qkv, qkv, later_mat, gain)


def _fox_attn_call(qkv, c_tiles, gain, B, S, W):
    T = B * S
    nq, q_spec, k_spec, v_spec, g_spec, o_spec = _attn_specs(B, S, W)
    rows = c_tiles.shape[1]
    return pl.pallas_call(
        _fox_attn_kernel,
        out_shape=jax.ShapeDtypeStruct((T, W), BF16),
        grid=(B, nq),
        in_specs=[q_spec, k_spec, v_spec,
                  pl.BlockSpec((1, rows, ATT_K), lambda b, i: (b, 0, 0)), g_spec],
        out_specs=o_spec,
        compiler_params=_cparams(("parallel", "arbitrary"), VMEM_LIMIT),
        name="fox_attn",
    )(qkv, qkv, qkv, c_tiles, gain)


HALO = 8


def _gelu_tanh(x):
    c = math.sqrt(2.0 / math.pi)
    return 0.5 * x * (1.0 + jnp.tanh(c * (x + 0.044715 * (x * x * x))))


def _conv_sgu_kernel(cv_ref, halo_ref, sg_ref, cw_ref, cb_ref, lg_ref, lb_ref, ws_ref, bs_ref,
                     gc_ref, gd_ref, oc_ref, od_ref, *, tiles_per_seq):
    R, W3 = cv_ref.shape
    W = W3 // 3
    i = pl.program_id(0)
    z = cv_ref[:, 2 * W:3 * W] * cv_ref[:, 0:W]
    zh = halo_ref[:, 2 * W:3 * W] * halo_ref[:, 0:W]
    zh = jnp.where(i % tiles_per_seq == 0, jnp.zeros_like(zh), zh)
    zf = jnp.concatenate([zh, z], axis=0)
    z1 = pltpu.roll(zf, 1, 0)[HALO:]
    z2 = pltpu.roll(zf, 2, 0)[HALO:]
    y = cb_ref[...] + cw_ref[0:1, :] * z2
    y = y + cw_ref[1:2, :] * z1
    y = y + cw_ref[2:3, :] * z
    _rms_group_store(oc_ref, cv_ref[:, W:2 * W] * y, gc_ref)
    gel = _gelu_tanh(sg_ref[...])
    u = gel[:, 0:W]
    vn = _layer_norm_rows(gel[:, W:2 * W], lg_ref[...], lb_ref[...]).astype(BF16)
    lane = lax.broadcasted_iota(jnp.int32, (1, W), 1)
    pi = lax.broadcasted_iota(jnp.int32, (SGU_BLOCK, SGU_BLOCK), 0) // CHUNK
    pj = lax.broadcasted_iota(jnp.int32, (SGU_BLOCK, SGU_BLOCK), 1) // CHUNK
    w_m = [jnp.where(pj <= pi, ws_ref[g], 0.0).astype(BF16) for g in range(HEADS)]
    blocks = []
    for n in range(R // SGU_BLOCK):
        vb = vn[n * SGU_BLOCK:(n + 1) * SGU_BLOCK, :]
        mixed = jnp.dot(w_m[0], vb, preferred_element_type=F32)
        for g in range(1, HEADS):
            mg = jnp.dot(w_m[g], vb, preferred_element_type=F32)
            mixed = jnp.where(lane >= g * HEAD_DIM, mg, mixed)
        blocks.append(mixed + bs_ref[...])
    mixed = jnp.concatenate(blocks, axis=0)
    _rms_group_store(od_ref, u * mixed, gd_ref)


def _conv_sgu_call(cv, sg, conv_w, conv_b, ln_g, ln_b, sgu_w, sgu_b, g_c, g_d, S, W):
    T = cv.shape[0]
    R = ROW_TILE
    tiles_per_seq = S // R
    bias_tile = jnp.repeat(sgu_b.T, HEAD_DIM, axis=1)
    row = lambda a: a.reshape(1, W)
    const = lambda shape: pl.BlockSpec(shape, lambda i: tuple(0 for _ in shape))
    return pl.pallas_call(
        functools.partial(_conv_sgu_kernel, tiles_per_seq=tiles_per_seq),
        out_shape=[jax.ShapeDtypeStruct((T, W), BF16)] * 2,
        grid=(T // R,),
        in_specs=[pl.BlockSpec((R, 3 * W), lambda i: (i, 0)),
                  pl.BlockSpec((HALO, 3 * W), lambda i: (jnp.maximum(i * (R // HALO) - 1, 0), 0)),
                  pl.BlockSpec((R, 2 * W), lambda i: (i, 0)),
                  const((CONV_WIDTH, W)), const((1, W)), const((1, W)), const((1, W)),
                  const((HEADS, SGU_BLOCK, SGU_BLOCK)), const((SGU_BLOCK, W)),
                  const((1, W)), const((1, W))],
        out_specs=[pl.BlockSpec((R, W), lambda i: (i, 0))] * 2,
        compiler_params=_cparams(("parallel",), VMEM_LIMIT),
        name="conv_sgu",
    )(cv, cv, sg, conv_w, row(conv_b), row(ln_g), row(ln_b), sgu_w, bias_tile, row(g_c), row(g_d))


def _outproj_router_kernel(ma_ref, mb_ref, mc_ref, md_ref, wo_ref, x_ref, g_ref, b_ref,
                           rw_ref, rb_ref, x1_ref, route_ref, *, alpha):
    W = ma_ref.shape[1]
    acc = jnp.dot(ma_ref[...], wo_ref[0:W, :], preferred_element_type=F32)
    for k, ref in enumerate((mb_ref, mc_ref, md_ref), start=1):
        acc = acc + jnp.dot(ref[...], wo_ref[k * W:(k + 1) * W, :], preferred_element_type=F32)
    x1 = _layer_norm_rows(alpha * x_ref[...] + acc, g_ref[...], b_ref[...])
    x1_ref[...] = x1
    logits = jnp.dot(x1, rw_ref[...], preferred_element_type=F32,
                     precision=lax.Precision.HIGHEST) + rb_ref[...]
    R = logits.shape[0]
    lane = lax.broadcasted_iota(jnp.int32, (R, ROUTE_LANES), 1).astype(F32)
    big = float(ROUTE_LANES)
    ninf = jnp.finfo(F32).min
    gmask = lane < N_EXPERT_GROUPS
    lg = jnp.where(gmask, logits, ninf)
    mg = jnp.max(lg, axis=-1, keepdims=True)
    sum_g = jnp.sum(jnp.where(gmask, jnp.exp(lg - mg), 0.0), axis=-1, keepdims=True)
    p_gsel = 1.0 / sum_g
    g_sel = jnp.min(jnp.where(lg == mg, lane, big), axis=-1, keepdims=True)
    e_lo = N_EXPERT_GROUPS + EXPERTS_PER_GROUP * g_sel
    emask = (lane >= e_lo) & (lane < e_lo + EXPERTS_PER_GROUP)
    le = jnp.where(emask, logits, ninf)
    me = jnp.max(le, axis=-1, keepdims=True)
    ee = jnp.where(emask, jnp.exp(le - me), 0.0)
    pe = ee / jnp.sum(ee, axis=-1, keepdims=True)
    pe = jnp.where(emask, pe, -1.0)
    p1 = jnp.max(pe, axis=-1, keepdims=True)
    i1 = jnp.min(jnp.where(pe == p1, lane, big), axis=-1, keepdims=True)
    pe2 = jnp.where(lane == i1, -1.0, pe)
    p2 = jnp.max(pe2, axis=-1, keepdims=True)
    i2 = jnp.min(jnp.where(pe2 == p2, lane, big), axis=-1, keepdims=True)
    denom = p1 + p2
    gate1 = p_gsel * p1 / denom
    gate2 = p_gsel * p2 / denom
    e1 = i1 - N_EXPERT_GROUPS
    e2 = i2 - N_EXPERT_GROUPS
    route = jnp.where(lane == 0, e1, jnp.where(lane == 1, e2, jnp.where(lane == 2, gate1, gate2)))
    route_ref[...] = route


def _outproj_router_call(mixes, w_out, x, g, b, rw, rb, alpha):
    T, D = x.shape
    W = mixes[0].shape[1]
    R = ROW_TILE
    const = lambda shape: pl.BlockSpec(shape, lambda i: tuple(0 for _ in shape))
    return pl.pallas_call(
        functools.partial(_outproj_router_kernel, alpha=alpha),
        out_shape=[jax.ShapeDtypeStruct((T, D), F32), jax.ShapeDtypeStruct((T, ROUTE_LANES), F32)],
        grid=(T // R,),
        in_specs=[pl.BlockSpec((R, W), lambda i: (i, 0))] * 4
        + [const((D, D)), pl.BlockSpec((R, D), lambda i: (i, 0)), const((1, D)), const((1, D)),
           const((D, ROUTE_LANES)), const((1, ROUTE_LANES))],
        out_specs=[pl.BlockSpec((R, D), lambda i: (i, 0)),
                   pl.BlockSpec((R, ROUTE_LANES), lambda i: (i, 0))],
        compiler_params=_cparams(("parallel",), VMEM_LIMIT),
        name="outproj_router",
    )(*mixes, w_out, x, g.reshape(1, D), b.reshape(1, D), rw, rb)


IDX_SLOTS = 3


def _experts_kernel(bexp_ref, nact_ref, idx_hbm, x_hbm, w1_ref, w3_ref, w2_ref, y_hbm,
                    idx_smem, xbuf, ybuf, sem_idx, sem_g, sem_s):
    i = pl.program_id(0)
    nact = nact_ref[0]
    RB = xbuf.shape[1]

    def idx_copy(blk):
        s = blk % IDX_SLOTS
        return pltpu.make_async_copy(idx_hbm.at[blk], idx_smem.at[s], sem_idx.at[s])

    def issue_gather(blk):
        s3 = blk % IDX_SLOTS
        s2 = blk % 2

        def body(r, _):
            tok = idx_smem[s3, r]
            pltpu.make_async_copy(x_hbm.at[pl.ds(tok, 1), :], xbuf.at[s2, pl.ds(r, 1), :],
                                  sem_g.at[s2]).start()
            return 0

        lax.fori_loop(0, RB, body, 0, unroll=8)

    def issue_scatter(blk):
        s3 = blk % IDX_SLOTS
        s2 = blk % 2

        def body(r, _):
            dst = idx_smem[s3, RB + r]
            pltpu.make_async_copy(ybuf.at[s2, pl.ds(r, 1), :], y_hbm.at[pl.ds(dst, 1), :],
                                  sem_s.at[s2]).start()
            return 0

        lax.fori_loop(0, RB, body, 0, unroll=8)

    def wait_rows(buf, sem, s2):
        pltpu.make_async_copy(buf.at[s2], buf.at[s2], sem.at[s2]).wait()

    @pl.when(i == 0)
    def _():
        c0 = idx_copy(0)
        c0.start()
        n_real = y_hbm.shape[0] - 2 * RB
        ybuf[...] = jnp.zeros(ybuf.shape, ybuf.dtype)
        fills = [pltpu.make_async_copy(ybuf.at[s], y_hbm.at[pl.ds(n_real + s * RB, RB), :], sem_s.at[s])
                 for s in range(2)]
        for f in fills:
            f.start()
        for f in fills:
            f.wait()
        c0.wait()
        issue_gather(0)

        @pl.when(nact > 1)
        def _():
            idx_copy(1).start()

    @pl.when(i < nact)
    def _():
        s2 = i % 2

        @pl.when(i + 1 < nact)
        def _():
            idx_copy(i + 1).wait()
            issue_gather(i + 1)

        @pl.when(i + 2 < nact)
        def _():
            idx_copy(i + 2).start()

        wait_rows(xbuf, sem_g, s2)
        xb = xbuf[s2].astype(BF16)
        h1 = jnp.dot(xb, w1_ref[0], preferred_element_type=F32)
        h3 = jnp.dot(xb, w3_ref[0], preferred_element_type=F32)
        h = (h1 * jax.nn.sigmoid(h1)) * h3
        ybuf[s2] = jnp.dot(h.astype(BF16), w2_ref[0], preferred_element_type=F32)

        @pl.when(i > 0)
        def _():
            wait_rows(ybuf, sem_s, 1 - s2)

        issue_scatter(i)

        @pl.when(i == nact - 1)
        def _():
            wait_rows(ybuf, sem_s, s2)


def _experts_call(block_expert, nact, idx, x1, w1, w3, w2, n_slots):
    T, D = x1.shape
    n_blocks = idx.shape[0]
    RB = EXPERT_ROWS
    DE = w1.shape[2]
    grid_spec = pltpu.PrefetchScalarGridSpec(
        num_scalar_prefetch=2,
        grid=(n_blocks,),
        in_specs=[pl.BlockSpec(memory_space=pl.ANY),
                  pl.BlockSpec(memory_space=pl.ANY),
                  pl.BlockSpec((1, D, DE), lambda i, be, na: (be[i], 0, 0)),
                  pl.BlockSpec((1, D, DE), lambda i, be, na: (be[i], 0, 0)),
                  pl.BlockSpec((1, DE, D), lambda i, be, na: (be[i], 0, 0))],
        out_specs=pl.BlockSpec(memory_space=pl.ANY),
        scratch_shapes=[pltpu.SMEM((IDX_SLOTS, 2 * RB), jnp.int32),
                        pltpu.VMEM((2, RB, D), F32),
                        pltpu.VMEM((2, RB, D), F32),
                        pltpu.SemaphoreType.DMA((IDX_SLOTS,)),
                        pltpu.SemaphoreType.DMA((2,)),
                        pltpu.SemaphoreType.DMA((2,))],
    )
    return pl.pallas_call(
        _experts_kernel,
        out_shape=jax.ShapeDtypeStruct((n_slots, D), F32),
        grid_spec=grid_spec,
        compiler_params=_cparams(("arbitrary",), VMEM_LIMIT),
        name="experts",
    )(block_expert, nact, idx, x1, w1, w3, w2)


def _combine_kernel(y_ref, route_ref, x_ref, g_ref, b_ref, o_ref, *, alpha):
    D = x_ref.shape[1]
    g0 = route_ref[:, 2:3]
    g1 = route_ref[:, 3:4]
    ffn = y_ref[:, 0:D] * g0 + y_ref[:, D:2 * D] * g1
    o_ref[...] = _layer_norm_rows(alpha * x_ref[...] + ffn, g_ref[...], b_ref[...])


def _combine_call(y_pairs, route, x1, g, b, alpha):
    T, D = x1.shape
    R = ROW_TILE
    const = lambda shape: pl.BlockSpec(shape, lambda i: tuple(0 for _ in shape))
    return pl.pallas_call(
        functools.partial(_combine_kernel, alpha=alpha),
        out_shape=jax.ShapeDtypeStruct((T, D), F32),
        grid=(T // R,),
        in_specs=[pl.BlockSpec((R, 2 * D), lambda i: (i, 0)),
                  pl.BlockSpec((R, ROUTE_LANES), lambda i: (i, 0)),
                  pl.BlockSpec((R, D), lambda i: (i, 0)), const((1, D)), const((1, D))],
        out_specs=pl.BlockSpec((R, D), lambda i: (i, 0)),
        compiler_params=_cparams(("parallel",), VMEM_LIMIT),
        name="combine_ln",
    )(y_pairs, route, x1, g.reshape(1, D), b.reshape(1, D))


def _dispatch_tables(route, T):
    RB = EXPERT_ROWS
    A = T * TOP_K
    e_flat = route[:, 0:TOP_K].astype(jnp.int32).reshape(A)
    onehot = (e_flat[:, None] == jnp.arange(N_EXPERTS, dtype=jnp.int32)[None, :]).astype(jnp.int32)
    csum = jnp.cumsum(onehot, axis=0)
    rank = jnp.take_along_axis(csum, e_flat[:, None], axis=1)[:, 0] - 1
    counts = csum[-1]
    padded = (counts + RB - 1) // RB * RB
    p_ends = jnp.cumsum(padded)
    p_starts = p_ends - padded
    dest = p_starts[e_flat] + rank
    P = -(-A // RB) * RB + N_EXPERTS * RB
    n_blocks = P // RB
    a_ids = jnp.arange(A, dtype=jnp.int32)
    src_tok = jnp.zeros((P,), jnp.int32).at[dest].set(a_ids // TOP_K)
    rows = jnp.arange(P, dtype=jnp.int32)
    trash = A + ((rows // RB) % 2) * RB + rows % RB
    dst_slot = trash.at[dest].set(a_ids)
    idx = jnp.concatenate([src_tok.reshape(n_blocks, RB), dst_slot.reshape(n_blocks, RB)], axis=1)
    block_expert = jnp.clip(
        jnp.searchsorted(p_ends, jnp.arange(n_blocks, dtype=jnp.int32) * RB, side='right'),
        0, N_EXPERTS - 1).astype(jnp.int32)
    nact = (p_ends[-1] // RB).astype(jnp.int32).reshape(1)
    return idx, block_expert, nact, A + 2 * RB


def kernel(x, ln_in_g, ln_in_b, w_in, b_f, conv_w, conv_b, sgu_ln_g, sgu_ln_b, sgu_w, sgu_b, grp_g, w_out, ln1_g, ln1_b, router_g_w, router_g_b, router_e_w, router_e_b, w1, w3, w2, ln2_g, ln2_b):
    B, S, D = x.shape
    depth = w_in.shape[0]
    W = D // N_MIXERS
    T = B * S
    alpha = (2.0 * depth) ** 0.25
    scale = 1.0 / math.sqrt(HEAD_DIM)

    xs = _layer_norm_call(x.reshape(T, D), ln_in_g, ln_in_b)
    for l in range(depth):
        wl = w_in[l]
        o_f = 6 * W
        o_cv = o_f + HEADS
        o_sg = o_cv + 3 * W
        f_cols = jnp.pad(wl[:, o_f:o_cv], ((0, 0), (0, LANES - HEADS)))
        w_all = jnp.concatenate(
            [wl[:, 0:W] * scale, wl[:, W:3 * W], wl[:, 3 * W:4 * W] * scale, wl[:, 4 * W:6 * W],
             wl[:, o_cv:o_sg], wl[:, o_sg:], f_cols], axis=1).astype(BF16)
        rw = jnp.pad(jnp.concatenate([router_g_w[l], router_e_w[l]], axis=1),
                     ((0, 0), (0, ROUTE_LANES - N_EXPERT_GROUPS - N_EXPERTS)))
        rb = jnp.pad(jnp.concatenate([router_g_b[l], router_e_b[l]]),
                     (0, ROUTE_LANES - N_EXPERT_GROUPS - N_EXPERTS)).reshape(1, ROUTE_LANES)
        gains = [grp_g[l, k * W:(k + 1) * W].reshape(1, W) for k in range(N_MIXERS)]

        qkv_a, qkv_b, cv, sg, f_pad = _inproj_call(xs, w_all, W)
        f_rows = f_pad[:, 0:HEADS].reshape(B, S, HEADS).transpose(0, 2, 1).reshape(B * HEADS, S)
        b_rows = jnp.tile(b_f[l], B).reshape(B * HEADS, 1)
        c = _fcum_call(f_rows, b_rows)
        c_tiles = c.reshape(B, HEADS * (S // ATT_K), ATT_K)
        mix_a = _sb_attn_call(qkv_a, gains[0], B, S, W)
        mix_b = _fox_attn_call(qkv_b, c_tiles, gains[1], B, S, W)
        mix_c, mix_d = _conv_sgu_call(cv, sg, conv_w[l], conv_b[l], sgu_ln_g[l], sgu_ln_b[l],
                                      sgu_w[l], sgu_b[l], gains[2][0], gains[3][0], S, W)
        x1, route = _outproj_router_call([mix_a, mix_b, mix_c, mix_d], w_out[l].astype(BF16), xs,
                                         ln1_g[l], ln1_b[l], rw, rb, alpha)
        idx, block_expert, nact, n_slots = _dispatch_tables(route, T)
        y = _experts_call(block_expert, nact, idx, x1, w1[l].astype(BF16), w3[l].astype(BF16),
                          w2[l].astype(BF16), n_slots)
        xs = _combine_call(y.reshape(n_slots // TOP_K, TOP_K * D), route, x1, ln2_g[l], ln2_b[l], alpha)
    return xs.reshape(B, S, D)
```

```python
import functools
import math

import jax
import jax.numpy as jnp
from jax import lax
from jax.experimental import pallas as pl
from jax.experimental.pallas import tpu as pltpu

N_MIXERS = 4
HEAD_DIM = 64
HEADS = 4
CONV_WIDTH = 3
SGU_BLOCK = 128
CHUNK = 64
N_EXPERT_GROUPS = 4
EXPERTS_PER_GROUP = 8
N_EXPERTS = N_EXPERT_GROUPS * EXPERTS_PER_GROUP
TOP_K = 2
LN_EPS = 1e-5
RMS_EPS = 1e-6

LANES = 128
V7X_VMEM_BYTES = 64 * 1024 * 1024
VMEM_LIMIT = 48 * 1024 * 1024

ROW_TILE = 512
ATT_Q = 256
ATT_K = 256
EXPERT_ROWS = 256
ROUTE_LANES = LANES

F32 = jnp.float32
BF16 = jnp.bfloat16


def _cparams(sem, vmem=None):
    return pltpu.CompilerParams(dimension_semantics=sem, vmem_limit_bytes=vmem)


def _layer_norm_rows(y, g, b):
    mu = jnp.mean(y, axis=-1, keepdims=True)
    d = y - mu
    var = jnp.mean(d * d, axis=-1, keepdims=True)
    return d * lax.rsqrt(var + LN_EPS) * g + b


def _softplus(z):
    return jnp.maximum(z, 0.0) + jnp.log1p(jnp.exp(-jnp.abs(z)))


def _ln_kernel(x_ref, g_ref, b_ref, o_ref):
    o_ref[...] = _layer_norm_rows(x_ref[...], g_ref[...], b_ref[...])


def _layer_norm_call(x, g, b):
    T, D = x.shape
    return pl.pallas_call(
        _ln_kernel,
        out_shape=jax.ShapeDtypeStruct((T, D), F32),
        grid=(T // ROW_TILE,),
        in_specs=[pl.BlockSpec((ROW_TILE, D), lambda i: (i, 0)),
                  pl.BlockSpec((1, D), lambda i: (0, 0)),
                  pl.BlockSpec((1, D), lambda i: (0, 0))],
        out_specs=pl.BlockSpec((ROW_TILE, D), lambda i: (i, 0)),
        compiler_params=_cparams(("parallel",)),
        name="ln_in",
    )(x, g.reshape(1, D), b.reshape(1, D))


def _inproj_kernel(x_ref, w_ref, qa_ref, qb_ref, cv_ref, sg_ref, f_ref, *, widths):
    xb = x_ref[...].astype(BF16)
    off = 0
    for ref, wd in zip((qa_ref, qb_ref, cv_ref, sg_ref, f_ref), widths):
        ref[...] = jnp.dot(xb, w_ref[:, off:off + wd], preferred_element_type=F32).astype(ref.dtype)
        off += wd


def _inproj_call(x, w_all, W):
    T, D = x.shape
    widths = (3 * W, 3 * W, 3 * W, 2 * W, LANES)
    NC = sum(widths)
    dts = (BF16, BF16, F32, F32, F32)
    return pl.pallas_call(
        functools.partial(_inproj_kernel, widths=widths),
        out_shape=[jax.ShapeDtypeStruct((T, wd), dt) for wd, dt in zip(widths, dts)],
        grid=(T // ROW_TILE,),
        in_specs=[pl.BlockSpec((ROW_TILE, D), lambda i: (i, 0)),
                  pl.BlockSpec((D, NC), lambda i: (0, 0))],
        out_specs=[pl.BlockSpec((ROW_TILE, wd), lambda i: (i, 0)) for wd in widths],
        compiler_params=_cparams(("parallel",), VMEM_LIMIT),
        name="inproj",
    )(x, w_all)


def _fcum_kernel(f_ref, b_ref, tri_ref, c_ref):
    R, S = f_ref.shape
    y = f_ref[...] + b_ref[...]
    lf = -_softplus(-y)
    tri = tri_ref[...]
    carry = jnp.zeros((R, 1), F32)
    for blk in range(S // LANES):
        seg = lf[:, blk * LANES:(blk + 1) * LANES]
        s1 = seg.astype(BF16)
        r1 = seg - s1.astype(F32)
        s2 = r1.astype(BF16)
        s3 = (r1 - s2.astype(F32)).astype(BF16)
        cs = (jnp.dot(s1, tri, preferred_element_type=F32)
              + jnp.dot(s2, tri, preferred_element_type=F32)
              + jnp.dot(s3, tri, preferred_element_type=F32)) + carry
        c_ref[:, blk * LANES:(blk + 1) * LANES] = cs
        carry = cs[:, LANES - 1:LANES]


def _fcum_call(f_rows, b_rows):
    R, S = f_rows.shape
    idx = jnp.arange(LANES)
    tri = (idx[:, None] <= idx[None, :]).astype(BF16)
    return pl.pallas_call(
        _fcum_kernel,
        out_shape=jax.ShapeDtypeStruct((R, S), F32),
        name="forget_cumsum",
    )(f_rows, b_rows, tri)


def _rms_group_store(o_ref, out, g_ref):
    ms = jnp.mean(out * out, axis=-1, keepdims=True)
    o_ref[...] = (out * lax.rsqrt(ms + RMS_EPS) * g_ref[...]).astype(o_ref.dtype)


def _head_masks():
    lane = lax.broadcasted_iota(jnp.int32, (1, LANES), 1)
    return [lane < HEAD_DIM, lane >= HEAD_DIM]


def _nt_dot(a, b):
    return lax.dot_general(a, b, (((1,), (1,)), ((), ())), preferred_element_type=F32)


def _neg_softplus(z):
    return jnp.minimum(-z, 0.0) - jnp.log(1.0 + jnp.exp(-jnp.abs(z)))


def _sb_attn_kernel(q_ref, k_ref, v_ref, m_ref, g_ref, o_ref):
    i = pl.program_id(1)
    Q = q_ref.shape[0]
    row = lax.broadcasted_iota(jnp.int32, (Q, ATT_K), 0)
    col = lax.broadcasted_iota(jnp.int32, (Q, ATT_K), 1)
    diag_mask = col < row
    hmasks = _head_masks()

    def tile(qms, j, state, mask):
        ks = pl.multiple_of(j * ATT_K, ATT_K)
        k_ts = [k_ref[pl.ds(ks, ATT_K), p * LANES:(p + 1) * LANES] for p in range(HEADS // 2)]
        v_ts = [v_ref[pl.ds(ks, ATT_K), p * LANES:(p + 1) * LANES] for p in range(HEADS // 2)]
        later_mat = m_ref[...]
        zs = [_nt_dot(qms[h], k_ts[h // 2]) for h in range(HEADS)]
        mid = []
        for h in range(HEADS):
            log_1m = _neg_softplus(zs[h])
            log_b = zs[h] + log_1m
            if mask is not None:
                log_1m = jnp.where(mask, log_1m, 0.0)
            hi = log_1m.astype(BF16)
            lo = (log_1m - hi.astype(F32)).astype(BF16)
            later = (jnp.dot(hi, later_mat, preferred_element_type=F32)
                     + jnp.dot(lo, later_mat, preferred_element_type=F32)) + state[2 * h]
            mid.append((log_b, later, log_1m[:, 0:1]))
        out = []
        for h in range(HEADS):
            log_b, later, first = mid[h]
            w = jnp.exp(log_b + later)
            if mask is not None:
                w = jnp.where(mask, w, 0.0)
            acc = state[2 * h + 1] + jnp.dot(w.astype(BF16), v_ts[h // 2], preferred_element_type=F32)
            out.extend((later[:, 0:1] + first, acc))
        return tuple(out)

    qms = []
    for h in range(HEADS):
        q_pair = q_ref[:, (h // 2) * LANES:(h // 2 + 1) * LANES]
        qms.append(jnp.where(hmasks[h % 2], q_pair, jnp.zeros_like(q_pair)))
    state = (jnp.zeros((Q, 1), F32), jnp.zeros((Q, LANES), F32)) * HEADS
    state = tile(qms, i, state, diag_mask)
    state = lax.fori_loop(1, i + 1, lambda jj, st: tile(qms, i - jj, st, None), state)
    outs = [jnp.where(hmasks[0], state[4 * p + 1], state[4 * p + 3]) for p in range(HEADS // 2)]
    _rms_group_store(o_ref, jnp.concatenate(outs, axis=-1), g_ref)


def _fox_attn_kernel(q_ref, k_ref, v_ref, c_ref, g_ref, o_ref):
    i = pl.program_id(1)
    Q = q_ref.shape[0]
    nk = c_ref.shape[1] // HEADS
    row = lax.broadcasted_iota(jnp.int32, (Q, ATT_K), 0)
    col = lax.broadcasted_iota(jnp.int32, (Q, ATT_K), 1)
    diag_mask = col <= row
    hmasks = _head_masks()
    neg = jnp.finfo(F32).min

    def tile(qms, j, state, mask):
        ks = pl.multiple_of(j * ATT_K, ATT_K)
        k_ts = [k_ref[pl.ds(ks, ATT_K), p * LANES:(p + 1) * LANES] for p in range(HEADS // 2)]
        v_ts = [v_ref[pl.ds(ks, ATT_K), p * LANES:(p + 1) * LANES] for p in range(HEADS // 2)]
        ss = []
        for h in range(HEADS):
            c_row = c_ref[0, pl.ds(h * nk + j, 1), :]
            s = _nt_dot(qms[h], k_ts[h // 2]) - c_row
            if mask is not None:
                s = jnp.where(mask, s, neg)
            ss.append(s)
        mid = []
        for h in range(HEADS):
            m, l = state[3 * h], state[3 * h + 1]
            m_new = jnp.maximum(m, jnp.max(ss[h], axis=-1, keepdims=True))
            alpha = jnp.exp(m - m_new)
            p = jnp.exp(ss[h] - m_new)
            l = alpha * l + jnp.sum(p, axis=-1, keepdims=True)
            mid.append((m_new, l, alpha, p.astype(BF16)))
        out = []
        for h in range(HEADS):
            m_new, l, alpha, p = mid[h]
            acc = alpha * state[3 * h + 2] + jnp.dot(p, v_ts[h // 2], preferred_element_type=F32)
            out.extend((m_new, l, acc))
        return tuple(out)

    qms = []
    for h in range(HEADS):
        q_pair = q_ref[:, (h // 2) * LANES:(h // 2 + 1) * LANES]
        qms.append(jnp.where(hmasks[h % 2], q_pair, jnp.zeros_like(q_pair)))
    state = (jnp.full((Q, 1), neg, F32), jnp.zeros((Q, 1), F32), jnp.zeros((Q, LANES), F32)) * HEADS
    state = tile(qms, i, state, diag_mask)
    state = lax.fori_loop(1, i + 1, lambda jj, st: tile(qms, i - jj, st, None), state)
    heads = [state[3 * h + 2] / state[3 * h + 1] for h in range(HEADS)]
    outs = [jnp.where(hmasks[0], heads[2 * p], heads[2 * p + 1]) for p in range(HEADS // 2)]
    _rms_group_store(o_ref, jnp.concatenate(outs, axis=-1), g_ref)


def _attn_specs(B, S, W):
    nq = S // ATT_Q
    q_spec = pl.BlockSpec((ATT_Q, W), lambda b, i: (b * nq + i, 0))
    k_spec = pl.BlockSpec((S, W), lambda b, i: (b, 1))
    v_spec = pl.BlockSpec((S, W), lambda b, i: (b, 2))
    g_spec = pl.BlockSpec((1, W), lambda b, i: (0, 0))
    o_spec = pl.BlockSpec((ATT_Q, W), lambda b, i: (b * nq + i, 0))
    return nq, q_spec, k_spec, v_spec, g_spec, o_spec


def _sb_attn_call(qkv, gain, B, S, W):
    T = B * S
    nq, q_spec, k_spec, v_spec, g_spec, o_spec = _attn_specs(B, S, W)
    idx = jnp.arange(ATT_K)
    later_mat = (idx[:, None] > idx[None, :]).astype(BF16)
    return pl.pallas_call(
        _sb_attn_kernel,
        out_shape=jax.ShapeDtypeStruct((T, W), BF16),
        grid=(B, nq),
        in_specs=[q_spec, k_spec, v_spec,
                  pl.BlockSpec((ATT_K, ATT_K), lambda b, i: (0, 0)), g_spec],
        out_specs=o_spec,
        compiler_params=_cparams(("parallel", "arbitrary"), VMEM_LIMIT),
        name="sb_attn",
    )(qkv, qkv, qkv, later_mat, gain)


def _fox_attn_call(qkv, c_tiles, gain, B, S, W):
    T = B * S
    nq, q_spec, k_spec, v_spec, g_spec, o_spec = _attn_specs(B, S, W)
    rows = c_tiles.shape[1]
    return pl.pallas_call(
        _fox_attn_kernel,
        out_shape=jax.ShapeDtypeStruct((T, W), BF16),
        grid=(B, nq),
        in_specs=[q_spec, k_spec, v_spec,
                  pl.BlockSpec((1, rows, ATT_K), lambda b, i: (b, 0, 0)), g_spec],
        out_specs=o_spec,
        compiler_params=_cparams(("parallel", "arbitrary"), VMEM_LIMIT),
        name="fox_attn",
    )(qkv, qkv, qkv, c_tiles, gain)


HALO = 8


def _gelu_tanh(x):
    c = math.sqrt(2.0 / math.pi)
    return 0.5 * x * (1.0 + jnp.tanh(c * (x + 0.044715 * (x * x * x))))


def _conv_sgu_kernel(cv_ref, halo_ref, sg_ref, cw_ref, cb_ref, lg_ref, lb_ref, ws_ref, bs_ref,
                     gc_ref, gd_ref, oc_ref, od_ref, *, tiles_per_seq):
    R, W3 = cv_ref.shape
    W = W3 // 3
    i = pl.program_id(0)
    z = cv_ref[:, 2 * W:3 * W] * cv_ref[:, 0:W]
    zh = halo_ref[:, 2 * W:3 * W] * halo_ref[:, 0:W]
    zh = jnp.where(i % tiles_per_seq == 0, jnp.zeros_like(zh), zh)
    zf = jnp.concatenate([zh, z], axis=0)
    z1 = pltpu.roll(zf, 1, 0)[HALO:]
    z2 = pltpu.roll(zf, 2, 0)[HALO:]
    y = cb_ref[...] + cw_ref[0:1, :] * z2
    y = y + cw_ref[1:2, :] * z1
    y = y + cw_ref[2:3, :] * z
    _rms_group_store(oc_ref, cv_ref[:, W:2 * W] * y, gc_ref)
    gel = _gelu_tanh(sg_ref[...])
    u = gel[:, 0:W]
    vn = _layer_norm_rows(gel[:, W:2 * W], lg_ref[...], lb_ref[...]).astype(BF16)
    lane = lax.broadcasted_iota(jnp.int32, (1, W), 1)
    pi = lax.broadcasted_iota(jnp.int32, (SGU_BLOCK, SGU_BLOCK), 0) // CHUNK
    pj = lax.broadcasted_iota(jnp.int32, (SGU_BLOCK, SGU_BLOCK), 1) // CHUNK
    w_m = [jnp.where(pj <= pi, ws_ref[g], 0.0).astype(BF16) for g in range(HEADS)]
    blocks = []
    for n in range(R // SGU_BLOCK):
        vb = vn[n * SGU_BLOCK:(n + 1) * SGU_BLOCK, :]
        mixed = jnp.dot(w_m[0], vb, preferred_element_type=F32)
        for g in range(1, HEADS):
            mg = jnp.dot(w_m[g], vb, preferred_element_type=F32)
            mixed = jnp.where(lane >= g * HEAD_DIM, mg, mixed)
        blocks.append(mixed + bs_ref[...])
    mixed = jnp.concatenate(blocks, axis=0)
    _rms_group_store(od_ref, u * mixed, gd_ref)


def _conv_sgu_call(cv, sg, conv_w, conv_b, ln_g, ln_b, sgu_w, sgu_b, g_c, g_d, S, W):
    T = cv.shape[0]
    R = ROW_TILE
    tiles_per_seq = S // R
    bias_tile = jnp.repeat(sgu_b.T, HEAD_DIM, axis=1)
    row = lambda a: a.reshape(1, W)
    const = lambda shape: pl.BlockSpec(shape, lambda i: tuple(0 for _ in shape))
    return pl.pallas_call(
        functools.partial(_conv_sgu_kernel, tiles_per_seq=tiles_per_seq),
        out_shape=[jax.ShapeDtypeStruct((T, W), BF16)] * 2,
        grid=(T // R,),
        in_specs=[pl.BlockSpec((R, 3 * W), lambda i: (i, 0)),
                  pl.BlockSpec((HALO, 3 * W), lambda i: (jnp.maximum(i * (R // HALO) - 1, 0), 0)),
                  pl.BlockSpec((R, 2 * W), lambda i: (i, 0)),
                  const((CONV_WIDTH, W)), const((1, W)), const((1, W)), const((1, W)),
                  const((HEADS, SGU_BLOCK, SGU_BLOCK)), const((SGU_BLOCK, W)),
                  const((1, W)), const((1, W))],
        out_specs=[pl.BlockSpec((R, W), lambda i: (i, 0))] * 2,
        compiler_params=_cparams(("parallel",), VMEM_LIMIT),
        name="conv_sgu",
    )(cv, cv, sg, conv_w, row(conv_b), row(ln_g), row(ln_b), sgu_w, bias_tile, row(g_c), row(g_d))


def _outproj_router_kernel(ma_ref, mb_ref, mc_ref, md_ref, wo_ref, x_ref, g_ref, b_ref,
                           rw_ref, rb_ref, x1_ref, route_ref, *, alpha):
    W = ma_ref.shape[1]
    acc = jnp.dot(ma_ref[...], wo_ref[0:W, :], preferred_element_type=F32)
    for k, ref in enumerate((mb_ref, mc_ref, md_ref), start=1):
        acc = acc + jnp.dot(ref[...], wo_ref[k * W:(k + 1) * W, :], preferred_element_type=F32)
    x1 = _layer_norm_rows(alpha * x_ref[...] + acc, g_ref[...], b_ref[...])
    x1_ref[...] = x1
    logits = jnp.dot(x1, rw_ref[...], preferred_element_type=F32,
                     precision=lax.Precision.HIGHEST) + rb_ref[...]
    R = logits.shape[0]
    lane = lax.broadcasted_iota(jnp.int32, (R, ROUTE_LANES), 1).astype(F32)
    big = float(ROUTE_LANES)
    ninf = jnp.finfo(F32).min
    gmask = lane < N_EXPERT_GROUPS
    lg = jnp.where(gmask, logits, ninf)
    mg = jnp.max(lg, axis=-1, keepdims=True)
    sum_g = jnp.sum(jnp.where(gmask, jnp.exp(lg - mg), 0.0), axis=-1, keepdims=True)
    p_gsel = 1.0 / sum_g
    g_sel = jnp.min(jnp.where(lg == mg, lane, big), axis=-1, keepdims=True)
    e_lo = N_EXPERT_GROUPS + EXPERTS_PER_GROUP * g_sel
    emask = (lane >= e_lo) & (lane < e_lo + EXPERTS_PER_GROUP)
    le = jnp.where(emask, logits, ninf)
    me = jnp.max(le, axis=-1, keepdims=True)
    ee = jnp.where(emask, jnp.exp(le - me), 0.0)
    pe = ee / jnp.sum(ee, axis=-1, keepdims=True)
    pe = jnp.where(emask, pe, -1.0)
    p1 = jnp.max(pe, axis=-1, keepdims=True)
    i1 = jnp.min(jnp.where(pe == p1, lane, big), axis=-1, keepdims=True)
    pe2 = jnp.where(lane == i1, -1.0, pe)
    p2 = jnp.max(pe2, axis=-1, keepdims=True)
    i2 = jnp.min(jnp.where(pe2 == p2, lane, big), axis=-1, keepdims=True)
    denom = p1 + p2
    gate1 = p_gsel * p1 / denom
    gate2 = p_gsel * p2 / denom
    e1 = i1 - N_EXPERT_GROUPS
    e2 = i2 - N_EXPERT_GROUPS
    route = jnp.where(lane == 0, e1, jnp.where(lane == 1, e2, jnp.where(lane == 2, gate1, gate2)))
    route_ref[...] = route


def _outproj_router_call(mixes, w_out, x, g, b, rw, rb, alpha):
    T, D = x.shape
    W = mixes[0].shape[1]
    R = ROW_TILE
    const = lambda shape: pl.BlockSpec(shape, lambda i: tuple(0 for _ in shape))
    return pl.pallas_call(
        functools.partial(_outproj_router_kernel, alpha=alpha),
        out_shape=[jax.ShapeDtypeStruct((T, D), F32), jax.ShapeDtypeStruct((T, ROUTE_LANES), F32)],
        grid=(T // R,),
        in_specs=[pl.BlockSpec((R, W), lambda i: (i, 0))] * 4
        + [const((D, D)), pl.BlockSpec((R, D), lambda i: (i, 0)), const((1, D)), const((1, D)),
           const((D, ROUTE_LANES)), const((1, ROUTE_LANES))],
        out_specs=[pl.BlockSpec((R, D), lambda i: (i, 0)),
                   pl.BlockSpec((R, ROUTE_LANES), lambda i: (i, 0))],
        compiler_params=_cparams(("parallel",), VMEM_LIMIT),
        name="outproj_router",
    )(*mixes, w_out, x, g.reshape(1, D), b.reshape(1, D), rw, rb)


IDX_SLOTS = 3


def _experts_kernel(bexp_ref, nact_ref, idx_hbm, x_hbm, w1_ref, w3_ref, w2_ref, y_hbm,
                    idx_smem, xbuf, ybuf, sem_idx, sem_g, sem_s):
    i = pl.program_id(0)
    nact = nact_ref[0]
    RB = xbuf.shape[1]

    def idx_copy(blk):
        s = blk % IDX_SLOTS
        return pltpu.make_async_copy(idx_hbm.at[blk], idx_smem.at[s], sem_idx.at[s])

    def issue_gather(blk):
        s3 = blk % IDX_SLOTS
        s2 = blk % 2

        def body(r, _):
            tok = idx_smem[s3, r]
            pltpu.make_async_copy(x_hbm.at[pl.ds(tok, 1), :], xbuf.at[s2, pl.ds(r, 1), :],
                                  sem_g.at[s2]).start()
            return 0

        lax.fori_loop(0, RB, body, 0, unroll=8)

    def issue_scatter(blk):
        s3 = blk % IDX_SLOTS
        s2 = blk % 2

        def body(r, _):
            dst = idx_smem[s3, RB + r]
            pltpu.make_async_copy(ybuf.at[s2, pl.ds(r, 1), :], y_hbm.at[pl.ds(dst, 1), :],
                                  sem_s.at[s2]).start()
            return 0

        lax.fori_loop(0, RB, body, 0, unroll=8)

    def wait_rows(buf, sem, s2):
        pltpu.make_async_copy(buf.at[s2], buf.at[s2], sem.at[s2]).wait()

    @pl.when(i == 0)
    def _():
        c0 = idx_copy(0)
        c0.start()
        n_real = y_hbm.shape[0] - 2 * RB
        ybuf[...] = jnp.zeros(ybuf.shape, ybuf.dtype)
        fills = [pltpu.make_async_copy(ybuf.at[s], y_hbm.at[pl.ds(n_real + s * RB, RB), :], sem_s.at[s])
                 for s in range(2)]
        for f in fills:
            f.start()
        for f in fills:
            f.wait()
        c0.wait()
        issue_gather(0)

        @pl.when(nact > 1)
        def _():
            idx_copy(1).start()

    @pl.when(i < nact)
    def _():
        s2 = i % 2

        @pl.when(i + 1 < nact)
        def _():
            idx_copy(i + 1).wait()
            issue_gather(i + 1)

        @pl.when(i + 2 < nact)
        def _():
            idx_copy(i + 2).start()

        wait_rows(xbuf, sem_g, s2)
        xb = xbuf[s2].astype(BF16)
        h1 = jnp.dot(xb, w1_ref[0], preferred_element_type=F32)
        h3 = jnp.dot(xb, w3_ref[0], preferred_element_type=F32)
        h = (h1 * jax.nn.sigmoid(h1)) * h3
        ybuf[s2] = jnp.dot(h.astype(BF16), w2_ref[0], preferred_element_type=F32)

        @pl.when(i > 0)
        def _():
            wait_rows(ybuf, sem_s, 1 - s2)

        issue_scatter(i)

        @pl.when(i == nact - 1)
        def _():
            wait_rows(ybuf, sem_s, s2)


def _experts_call(block_expert, nact, idx, x1, w1, w3, w2, n_slots):
    T, D = x1.shape
    n_blocks = idx.shape[0]
    RB = EXPERT_ROWS
    DE = w1.shape[2]
    grid_spec = pltpu.PrefetchScalarGridSpec(
        num_scalar_prefetch=2,
        grid=(n_blocks,),
        in_specs=[pl.BlockSpec(memory_space=pl.ANY),
                  pl.BlockSpec(memory_space=pl.ANY),
                  pl.BlockSpec((1, D, DE), lambda i, be, na: (be[i], 0, 0)),
                  pl.BlockSpec((1, D, DE), lambda i, be, na: (be[i], 0, 0)),
                  pl.BlockSpec((1, DE, D), lambda i, be, na: (be[i], 0, 0))],
        out_specs=pl.BlockSpec(memory_space=pl.ANY),
        scratch_shapes=[pltpu.SMEM((IDX_SLOTS, 2 * RB), jnp.int32),
                        pltpu.VMEM((2, RB, D), F32),
                        pltpu.VMEM((2, RB, D), F32),
                        pltpu.SemaphoreType.DMA((IDX_SLOTS,)),
                        pltpu.SemaphoreType.DMA((2,)),
                        pltpu.SemaphoreType.DMA((2,))],
    )
    return pl.pallas_call(
        _experts_kernel,
        out_shape=jax.ShapeDtypeStruct((n_slots, D), F32),
        grid_spec=grid_spec,
        compiler_params=_cparams(("arbitrary",), VMEM_LIMIT),
        name="experts",
    )(block_expert, nact, idx, x1, w1, w3, w2)


def _combine_kernel(y_ref, route_ref, x_ref, g_ref, b_ref, o_ref, *, alpha):
    D = x_ref.shape[1]
    g0 = route_ref[:, 2:3]
    g1 = route_ref[:, 3:4]
    ffn = y_ref[:, 0:D] * g0 + y_ref[:, D:2 * D] * g1
    o_ref[...] = _layer_norm_rows(alpha * x_ref[...] + ffn, g_ref[...], b_ref[...])


def _combine_call(y_pairs, route, x1, g, b, alpha):
    T, D = x1.shape
    R = ROW_TILE
    const = lambda shape: pl.BlockSpec(shape, lambda i: tuple(0 for _ in shape))
    return pl.pallas_call(
        functools.partial(_combine_kernel, alpha=alpha),
        out_shape=jax.ShapeDtypeStruct((T, D), F32),
        grid=(T // R,),
        in_specs=[pl.BlockSpec((R, 2 * D), lambda i: (i, 0)),
                  pl.BlockSpec((R, ROUTE_LANES), lambda i: (i, 0)),
                  pl.BlockSpec((R, D), lambda i: (i, 0)), const((1, D)), const((1, D))],
        out_specs=pl.BlockSpec((R, D), lambda i: (i, 0)),
        compiler_params=_cparams(("parallel",), VMEM_LIMIT),
        name="combine_ln",
    )(y_pairs, route, x1, g.reshape(1, D), b.reshape(1, D))


def _dispatch_tables(route, T):
    RB = EXPERT_ROWS
    A = T * TOP_K
    e_flat = route[:, 0:TOP_K].astype(jnp.int32).reshape(A)
    onehot = (e_flat[:, None] == jnp.arange(N_EXPERTS, dtype=jnp.int32)[None, :]).astype(jnp.int32)
    csum = jnp.cumsum(onehot, axis=0)
    rank = jnp.take_along_axis(csum, e_flat[:, None], axis=1)[:, 0] - 1
    counts = csum[-1]
    padded = (counts + RB - 1) // RB * RB
    p_ends = jnp.cumsum(padded)
    p_starts = p_ends - padded
    dest = p_starts[e_flat] + rank
    P = -(-A // RB) * RB + N_EXPERTS * RB
    n_blocks = P // RB
    a_ids = jnp.arange(A, dtype=jnp.int32)
    src_tok = jnp.zeros((P,), jnp.int32).at[dest].set(a_ids // TOP_K)
    rows = jnp.arange(P, dtype=jnp.int32)
    trash = A + ((rows // RB) % 2) * RB + rows % RB
    dst_slot = trash.at[dest].set(a_ids)
    idx = jnp.concatenate([src_tok.reshape(n_blocks, RB), dst_slot.reshape(n_blocks, RB)], axis=1)
    block_start = jnp.arange(n_blocks, dtype=jnp.int32) * RB
    block_expert = jnp.minimum(
        jnp.sum((p_ends[None, :] <= block_start[:, None]).astype(jnp.int32), axis=1), N_EXPERTS - 1)
    nact = (p_ends[-1] // RB).astype(jnp.int32).reshape(1)
    return idx, block_expert, nact, A + 2 * RB


def kernel(x, ln_in_g, ln_in_b, w_in, b_f, conv_w, conv_b, sgu_ln_g, sgu_ln_b, sgu_w, sgu_b, grp_g, w_out, ln1_g, ln1_b, router_g_w, router_g_b, router_e_w, router_e_b, w1, w3, w2, ln2_g, ln2_b):
    B, S, D = x.shape
    depth = w_in.shape[0]
    W = D // N_MIXERS
    T = B * S
    alpha = (2.0 * depth) ** 0.25
    scale = 1.0 / math.sqrt(HEAD_DIM)

    xs = _layer_norm_call(x.reshape(T, D), ln_in_g, ln_in_b)
    for l in range(depth):
        wl = w_in[l]
        o_f = 6 * W
        o_cv = o_f + HEADS
        o_sg = o_cv + 3 * W
        f_cols = jnp.pad(wl[:, o_f:o_cv], ((0, 0), (0, LANES - HEADS)))
        w_all = jnp.concatenate(
            [wl[:, 0:W] * scale, wl[:, W:3 * W], wl[:, 3 * W:4 * W] * scale, wl[:, 4 * W:6 * W],
             wl[:, o_cv:o_sg], wl[:, o_sg:], f_cols], axis=1).astype(BF16)
        rw = jnp.pad(jnp.concatenate([router_g_w[l], router_e_w[l]], axis=1),
                     ((0, 0), (0, ROUTE_LANES - N_EXPERT_GROUPS - N_EXPERTS)))
        rb = jnp.pad(jnp.concatenate([router_g_b[l], router_e_b[l]]),
                     (0, ROUTE_LANES - N_EXPERT_GROUPS - N_EXPERTS)).reshape(1, ROUTE_LANES)
        gains = [grp_g[l, k * W:(k + 1) * W].reshape(1, W) for k in range(N_MIXERS)]

        qkv_a, qkv_b, cv, sg, f_pad = _inproj_call(xs, w_all, W)
        f_rows = f_pad[:, 0:HEADS].reshape(B, S, HEADS).transpose(0, 2, 1).reshape(B * HEADS, S)
        b_rows = jnp.tile(b_f[l], B).reshape(B * HEADS, 1)
        c = _fcum_call(f_rows, b_rows)
        c_tiles = c.reshape(B, HEADS * (S // ATT_K), ATT_K)
        mix_a = _sb_attn_call(qkv_a, gains[0], B, S, W)
        mix_b = _fox_attn_call(qkv_b, c_tiles, gains[1], B, S, W)
        mix_c, mix_d = _conv_sgu_call(cv, sg, conv_w[l], conv_b[l], sgu_ln_g[l], sgu_ln_b[l],
                                      sgu_w[l], sgu_b[l], gains[2][0], gains[3][0], S, W)
        x1, route = _outproj_router_call([mix_a, mix_b, mix_c, mix_d], w_out[l].astype(BF16), xs,
                                         ln1_g[l], ln1_b[l], rw, rb, alpha)
        idx, block_expert, nact, n_slots = _dispatch_tables(route, T)
        y = _experts_call(block_expert, nact, idx, x1, w1[l].astype(BF16), w3[l].astype(BF16),
                          w2[l].astype(BF16), n_slots)
        xs = _combine_call(y.reshape(n_slots // TOP_K, TOP_K * D), route, x1, ln2_g[l], ln2_b[l], alpha)
    return xs.reshape(B, S, D)
```

```python
import functools
import math

import jax
import jax.numpy as jnp
from jax import lax
from jax.experimental import pallas as pl
from jax.experimental.pallas import tpu as pltpu

N_MIXERS = 4
HEAD_DIM = 64
HEADS = 4
CONV_WIDTH = 3
SGU_BLOCK = 128
CHUNK = 64
N_EXPERT_GROUPS = 4
EXPERTS_PER_GROUP = 8
N_EXPERTS = N_EXPERT_GROUPS * EXPERTS_PER_GROUP
TOP_K = 2
LN_EPS = 1e-5
RMS_EPS = 1e-6

LANES = 128
V7X_VMEM_BYTES = 64 * 1024 * 1024
VMEM_LIMIT = 48 * 1024 * 1024

ROW_TILE = 512
ATT_Q = 256
ATT_K = 256
EXPERT_ROWS = 256
ROUTE_LANES = LANES

F32 = jnp.float32
BF16 = jnp.bfloat16


def _cparams(sem, vmem=None):
    return pltpu.CompilerParams(dimension_semantics=sem, vmem_limit_bytes=vmem)


def _layer_norm_rows(y, g, b):
    mu = jnp.mean(y, axis=-1, keepdims=True)
    d = y - mu
    var = jnp.mean(d * d, axis=-1, keepdims=True)
    return d * lax.rsqrt(var + LN_EPS) * g + b


def _softplus(z):
    return jnp.maximum(z, 0.0) + jnp.log1p(jnp.exp(-jnp.abs(z)))


def _ln_kernel(x_ref, g_ref, b_ref, o_ref):
    o_ref[...] = _layer_norm_rows(x_ref[...], g_ref[...], b_ref[...])


def _layer_norm_call(x, g, b):
    T, D = x.shape
    return pl.pallas_call(
        _ln_kernel,
        out_shape=jax.ShapeDtypeStruct((T, D), F32),
        grid=(T // ROW_TILE,),
        in_specs=[pl.BlockSpec((ROW_TILE, D), lambda i: (i, 0)),
                  pl.BlockSpec((1, D), lambda i: (0, 0)),
                  pl.BlockSpec((1, D), lambda i: (0, 0))],
        out_specs=pl.BlockSpec((ROW_TILE, D), lambda i: (i, 0)),
        compiler_params=_cparams(("parallel",)),
        name="ln_in",
    )(x, g.reshape(1, D), b.reshape(1, D))


def _inproj_kernel(x_ref, w_ref, qa_ref, qb_ref, cv_ref, sg_ref, f_ref, *, widths):
    xb = x_ref[...].astype(BF16)
    off = 0
    for ref, wd in zip((qa_ref, qb_ref, cv_ref, sg_ref, f_ref), widths):
        ref[...] = jnp.dot(xb, w_ref[:, off:off + wd], preferred_element_type=F32).astype(ref.dtype)
        off += wd


def _inproj_call(x, w_all, W):
    T, D = x.shape
    widths = (3 * W, 3 * W, 3 * W, 2 * W, LANES)
    NC = sum(widths)
    dts = (BF16, BF16, F32, F32, F32)
    return pl.pallas_call(
        functools.partial(_inproj_kernel, widths=widths),
        out_shape=[jax.ShapeDtypeStruct((T, wd), dt) for wd, dt in zip(widths, dts)],
        grid=(T // ROW_TILE,),
        in_specs=[pl.BlockSpec((ROW_TILE, D), lambda i: (i, 0)),
                  pl.BlockSpec((D, NC), lambda i: (0, 0))],
        out_specs=[pl.BlockSpec((ROW_TILE, wd), lambda i: (i, 0)) for wd in widths],
        compiler_params=_cparams(("parallel",), VMEM_LIMIT),
        name="inproj",
    )(x, w_all)


def _fcum_kernel(f_ref, b_ref, tri_ref, c_ref):
    R, S = f_ref.shape
    y = f_ref[...] + b_ref[...]
    lf = -_softplus(-y)
    tri = tri_ref[...]
    carry = jnp.zeros((R, 1), F32)
    for blk in range(S // LANES):
        seg = lf[:, blk * LANES:(blk + 1) * LANES]
        s1 = seg.astype(BF16)
        r1 = seg - s1.astype(F32)
        s2 = r1.astype(BF16)
        s3 = (r1 - s2.astype(F32)).astype(BF16)
        cs = (jnp.dot(s1, tri, preferred_element_type=F32)
              + jnp.dot(s2, tri, preferred_element_type=F32)
              + jnp.dot(s3, tri, preferred_element_type=F32)) + carry
        c_ref[:, blk * LANES:(blk + 1) * LANES] = cs
        carry = cs[:, LANES - 1:LANES]


def _fcum_call(f_rows, b_rows):
    R, S = f_rows.shape
    idx = jnp.arange(LANES)
    tri = (idx[:, None] <= idx[None, :]).astype(BF16)
    return pl.pallas_call(
        _fcum_kernel,
        out_shape=jax.ShapeDtypeStruct((R, S), F32),
        name="forget_cumsum",
    )(f_rows, b_rows, tri)


def _rms_group_store(o_ref, out, g_ref):
    ms = jnp.mean(out * out, axis=-1, keepdims=True)
    o_ref[...] = (out * lax.rsqrt(ms + RMS_EPS) * g_ref[...]).astype(o_ref.dtype)


def _head_masks():
    lane = lax.broadcasted_iota(jnp.int32, (1, LANES), 1)
    return [lane < HEAD_DIM, lane >= HEAD_DIM]


def _nt_dot(a, b):
    return lax.dot_general(a, b, (((1,), (1,)), ((), ())), preferred_element_type=F32)


def _neg_softplus(z):
    return jnp.minimum(-z, 0.0) - jnp.log(1.0 + jnp.exp(-jnp.abs(z)))


def _sb_attn_kernel(q_ref, k_ref, v_ref, m_ref, g_ref, o_ref):
    i = pl.program_id(1)
    Q = q_ref.shape[0]
    row = lax.broadcasted_iota(jnp.int32, (Q, ATT_K), 0)
    col = lax.broadcasted_iota(jnp.int32, (Q, ATT_K), 1)
    diag_mask = col < row
    hmasks = _head_masks()

    def tile(qms, j, state, mask):
        ks = pl.multiple_of(j * ATT_K, ATT_K)
        k_ts = [k_ref[pl.ds(ks, ATT_K), p * LANES:(p + 1) * LANES] for p in range(HEADS // 2)]
        v_ts = [v_ref[pl.ds(ks, ATT_K), p * LANES:(p + 1) * LANES] for p in range(HEADS // 2)]
        later_mat = m_ref[...]
        zs = [_nt_dot(qms[h], k_ts[h // 2]) for h in range(HEADS)]
        mid = []
        for h in range(HEADS):
            log_1m = _neg_softplus(zs[h])
            log_b = zs[h] + log_1m
            if mask is not None:
                log_1m = jnp.where(mask, log_1m, 0.0)
            hi = log_1m.astype(BF16)
            lo = (log_1m - hi.astype(F32)).astype(BF16)
            later = (jnp.dot(hi, later_mat, preferred_element_type=F32)
                     + jnp.dot(lo, later_mat, preferred_element_type=F32)) + state[2 * h]
            mid.append((log_b, later, log_1m[:, 0:1]))
        out = []
        for h in range(HEADS):
            log_b, later, first = mid[h]
            w = jnp.exp(log_b + later)
            if mask is not None:
                w = jnp.where(mask, w, 0.0)
            acc = state[2 * h + 1] + jnp.dot(w.astype(BF16), v_ts[h // 2], preferred_element_type=F32)
            out.extend((later[:, 0:1] + first, acc))
        return tuple(out)

    qms = []
    for h in range(HEADS):
        q_pair = q_ref[:, (h // 2) * LANES:(h // 2 + 1) * LANES]
        qms.append(jnp.where(hmasks[h % 2], q_pair, jnp.zeros_like(q_pair)))
    state = (jnp.zeros((Q, 1), F32), jnp.zeros((Q, LANES), F32)) * HEADS
    state = tile(qms, i, state, diag_mask)
    state = lax.fori_loop(1, i + 1, lambda jj, st: tile(qms, i - jj, st, None), state)
    outs = [jnp.where(hmasks[0], state[4 * p + 1], state[4 * p + 3]) for p in range(HEADS // 2)]
    _rms_group_store(o_ref, jnp.concatenate(outs, axis=-1), g_ref)


def _fox_attn_kernel(q_ref, k_ref, v_ref, c_ref, g_ref, o_ref):
    i = pl.program_id(1)
    Q = q_ref.shape[0]
    nk = c_ref.shape[1] // HEADS
    row = lax.broadcasted_iota(jnp.int32, (Q, ATT_K), 0)
    col = lax.broadcasted_iota(jnp.int32, (Q, ATT_K), 1)
    diag_mask = col <= row
    hmasks = _head_masks()
    neg = jnp.finfo(F32).min

    def tile(qms, j, state, mask):
        ks = pl.multiple_of(j * ATT_K, ATT_K)
        k_ts = [k_ref[pl.ds(ks, ATT_K), p * LANES:(p + 1) * LANES] for p in range(HEADS // 2)]
        v_ts = [v_ref[pl.ds(ks, ATT_K), p * LANES:(p + 1) * LANES] for p in range(HEADS // 2)]
        ss = []
        for h in range(HEADS):
            c_row = c_ref[0, pl.ds(h * nk + j, 1), :]
            s = _nt_dot(qms[h], k_ts[h // 2]) - c_row
            if mask is not None:
                s = jnp.where(mask, s, neg)
            ss.append(s)
        mid = []
        for h in range(HEADS):
            m, l = state[3 * h], state[3 * h + 1]
            m_new = jnp.maximum(m, jnp.max(ss[h], axis=-1, keepdims=True))
            alpha = jnp.exp(m - m_new)
            p = jnp.exp(ss[h] - m_new)
            l = alpha * l + jnp.sum(p, axis=-1, keepdims=True)
            mid.append((m_new, l, alpha, p.astype(BF16)))
        out = []
        for h in range(HEADS):
            m_new, l, alpha, p = mid[h]
            acc = alpha * state[3 * h + 2] + jnp.dot(p, v_ts[h // 2], preferred_element_type=F32)
            out.extend((m_new, l, acc))
        return tuple(out)

    qms = []
    for h in range(HEADS):
        q_pair = q_ref[:, (h // 2) * LANES:(h // 2 + 1) * LANES]
        qms.append(jnp.where(hmasks[h % 2], q_pair, jnp.zeros_like(q_pair)))
    state = (jnp.full((Q, 1), neg, F32), jnp.zeros((Q, 1), F32), jnp.zeros((Q, LANES), F32)) * HEADS
    state = tile(qms, i, state, diag_mask)
    state = lax.fori_loop(1, i + 1, lambda jj, st: tile(qms, i - jj, st, None), state)
    heads = [state[3 * h + 2] / state[3 * h + 1] for h in range(HEADS)]
    outs = [jnp.where(hmasks[0], heads[2 * p], heads[2 * p + 1]) for p in range(HEADS // 2)]
    _rms_group_store(o_ref, jnp.concatenate(outs, axis=-1), g_ref)


def _attn_specs(B, S, W):
    nq = S // ATT_Q
    q_spec = pl.BlockSpec((ATT_Q, W), lambda b, i: (b * nq + i, 0))
    k_spec = pl.BlockSpec((S, W), lambda b, i: (b, 1))
    v_spec = pl.BlockSpec((S, W), lambda b, i: (b, 2))
    g_spec = pl.BlockSpec((1, W), lambda b, i: (0, 0))
    o_spec = pl.BlockSpec((ATT_Q, W), lambda b, i: (b * nq + i, 0))
    return nq, q_spec, k_spec, v_spec, g_spec, o_spec


def _sb_attn_call(qkv, gain, B, S, W):
    T = B * S
    nq, q_spec, k_spec, v_spec, g_spec, o_spec = _attn_specs(B, S, W)
    idx = jnp.arange(ATT_K)
    later_mat = (idx[:, None] > idx[None, :]).astype(BF16)
    return pl.pallas_call(
        _sb_attn_kernel,
        out_shape=jax.ShapeDtypeStruct((T, W), BF16),
        grid=(B, nq),
        in_specs=[q_spec, k_spec, v_spec,
                  pl.BlockSpec((ATT_K, ATT_K), lambda b, i: (0, 0)), g_spec],
        out_specs=o_spec,
        compiler_params=_cparams(("parallel", "arbitrary"), VMEM_LIMIT),
        name="sb_attn",
    )(qkv, qkv, qkv, later_mat, gain)


def _fox_attn_call(qkv, c_tiles, gain, B, S, W):
    T = B * S
    nq, q_spec, k_spec, v_spec, g_spec, o_spec = _attn_specs(B, S, W)
    rows = c_tiles.shape[1]
    return pl.pallas_call(
        _fox_attn_kernel,
        out_shape=jax.ShapeDtypeStruct((T, W), BF16),
        grid=(B, nq),
        in_specs=[q_spec, k_spec, v_spec,
                  pl.BlockSpec((1, rows, ATT_K), lambda b, i: (b, 0, 0)), g_spec],
        out_specs=o_spec,
        compiler_params=_cparams(("parallel", "arbitrary"), VMEM_LIMIT),
        name="fox_attn",
    )(qkv, qkv, qkv, c_tiles, gain)


HALO = 8


def _gelu_tanh(x):
    c = math.sqrt(2.0 / math.pi)
    return 0.5 * x * (1.0 + jnp.tanh(c * (x + 0.044715 * (x * x * x))))


def _conv_sgu_kernel(cv_ref, halo_ref, sg_ref, cw_ref, cb_ref, lg_ref, lb_ref, ws_ref, bs_ref,
                     gc_ref, gd_ref, oc_ref, od_ref, *, tiles_per_seq):
    R, W3 = cv_ref.shape
    W = W3 // 3
    i = pl.program_id(0)
    z = cv_ref[:, 2 * W:3 * W] * cv_ref[:, 0:W]
    zh = halo_ref[:, 2 * W:3 * W] * halo_ref[:, 0:W]
    zh = jnp.where(i % tiles_per_seq == 0, jnp.zeros_like(zh), zh)
    zf = jnp.concatenate([zh, z], axis=0)
    z1 = pltpu.roll(zf, 1, 0)[HALO:]
    z2 = pltpu.roll(zf, 2, 0)[HALO:]
    y = cb_ref[...] + cw_ref[0:1, :] * z2
    y = y + cw_ref[1:2, :] * z1
    y = y + cw_ref[2:3, :] * z
    _rms_group_store(oc_ref, cv_ref[:, W:2 * W] * y, gc_ref)
    gel = _gelu_tanh(sg_ref[...])
    u = gel[:, 0:W]
    vn = _layer_norm_rows(gel[:, W:2 * W], lg_ref[...], lb_ref[...]).astype(BF16)
    lane = lax.broadcasted_iota(jnp.int32, (1, W), 1)
    pi = lax.broadcasted_iota(jnp.int32, (SGU_BLOCK, SGU_BLOCK), 0) // CHUNK
    pj = lax.broadcasted_iota(jnp.int32, (SGU_BLOCK, SGU_BLOCK), 1) // CHUNK
    w_m = [jnp.where(pj <= pi, ws_ref[g], 0.0).astype(BF16) for g in range(HEADS)]
    blocks = []
    for n in range(R // SGU_BLOCK):
        vb = vn[n * SGU_BLOCK:(n + 1) * SGU_BLOCK, :]
        mixed = jnp.dot(w_m[0], vb, preferred_element_type=F32)
        for g in range(1, HEADS):
            mg = jnp.dot(w_m[g], vb, preferred_element_type=F32)
            mixed = jnp.where(lane >= g * HEAD_DIM, mg, mixed)
        blocks.append(mixed + bs_ref[...])
    mixed = jnp.concatenate(blocks, axis=0)
    _rms_group_store(od_ref, u * mixed, gd_ref)


def _conv_sgu_call(cv, sg, conv_w, conv_b, ln_g, ln_b, sgu_w, sgu_b, g_c, g_d, S, W):
    T = cv.shape[0]
    R = ROW_TILE
    tiles_per_seq = S // R
    bias_tile = jnp.repeat(sgu_b.T, HEAD_DIM, axis=1)
    row = lambda a: a.reshape(1, W)
    const = lambda shape: pl.BlockSpec(shape, lambda i: tuple(0 for _ in shape))
    return pl.pallas_call(
        functools.partial(_conv_sgu_kernel, tiles_per_seq=tiles_per_seq),
        out_shape=[jax.ShapeDtypeStruct((T, W), BF16)] * 2,
        grid=(T // R,),
        in_specs=[pl.BlockSpec((R, 3 * W), lambda i: (i, 0)),
                  pl.BlockSpec((HALO, 3 * W), lambda i: (jnp.maximum(i * (R // HALO) - 1, 0), 0)),
                  pl.BlockSpec((R, 2 * W), lambda i: (i, 0)),
                  const((CONV_WIDTH, W)), const((1, W)), const((1, W)), const((1, W)),
                  const((HEADS, SGU_BLOCK, SGU_BLOCK)), const((SGU_BLOCK, W)),
                  const((1, W)), const((1, W))],
        out_specs=[pl.BlockSpec((R, W), lambda i: (i, 0))] * 2,
        compiler_params=_cparams(("parallel",), VMEM_LIMIT),
        name="conv_sgu",
    )(cv, cv, sg, conv_w, row(conv_b), row(ln_g), row(ln_b), sgu_w, bias_tile, row(g_c), row(g_d))


ROUTE_CHUNK = ROW_TILE


def _outproj_router_kernel(ma_ref, mb_ref, mc_ref, md_ref, wo_ref, x_ref, g_ref, b_ref,
                           rwa_ref, rwh_ref, rb_ref, tri_ref, x1_ref, route_ref, cnt_ref, base_ref, *, alpha):
    @pl.when(pl.program_id(0) == 0)
    def _():
        base_ref[...] = jnp.zeros(base_ref.shape, base_ref.dtype)

    base = base_ref[...]
    for c in range(x_ref.shape[0] // ROUTE_CHUNK):
        rows = slice(c * ROUTE_CHUNK, (c + 1) * ROUTE_CHUNK)
        base = _outproj_route_rows(rows, ma_ref, mb_ref, mc_ref, md_ref, wo_ref, x_ref, g_ref, b_ref,
                                   rwa_ref, rwh_ref, rb_ref, tri_ref, x1_ref, route_ref, base, alpha)
    base_ref[...] = base
    cnt_ref[...] = jnp.broadcast_to(base, cnt_ref.shape)


def _outproj_route_rows(rows, ma_ref, mb_ref, mc_ref, md_ref, wo_ref, x_ref, g_ref, b_ref,
                        rwa_ref, rwh_ref, rb_ref, tri_ref, x1_ref, route_ref, base, alpha):
    W = ma_ref.shape[1]
    acc = jnp.dot(ma_ref[rows, :], wo_ref[0:W, :], preferred_element_type=F32)
    for k, ref in enumerate((mb_ref, mc_ref, md_ref), start=1):
        acc = acc + jnp.dot(ref[rows, :], wo_ref[k * W:(k + 1) * W, :], preferred_element_type=F32)
    x1 = _layer_norm_rows(alpha * x_ref[rows, :] + acc, g_ref[...], b_ref[...])
    x1_ref[rows, :] = x1
    xh = x1.astype(BF16)
    xl = (x1 - xh.astype(F32)).astype(BF16)
    both = jnp.dot(xh, rwa_ref[...], preferred_element_type=F32)
    logits = (both[:, 0:ROUTE_LANES] + both[:, ROUTE_LANES:2 * ROUTE_LANES]
              + jnp.dot(xl, rwh_ref[...], preferred_element_type=F32)) + rb_ref[...]
    R = logits.shape[0]
    lane = lax.broadcasted_iota(jnp.int32, (R, ROUTE_LANES), 1).astype(F32)
    big = float(ROUTE_LANES)
    ninf = jnp.finfo(F32).min
    gmask = lane < N_EXPERT_GROUPS
    lg = jnp.where(gmask, logits, ninf)
    mg = jnp.max(lg, axis=-1, keepdims=True)
    sum_g = jnp.sum(jnp.where(gmask, jnp.exp(lg - mg), 0.0), axis=-1, keepdims=True)
    p_gsel = 1.0 / sum_g
    g_sel = jnp.min(jnp.where(lg == mg, lane, big), axis=-1, keepdims=True)
    e_lo = N_EXPERT_GROUPS + EXPERTS_PER_GROUP * g_sel
    emask = (lane >= e_lo) & (lane < e_lo + EXPERTS_PER_GROUP)
    le = jnp.where(emask, logits, ninf)
    me = jnp.max(le, axis=-1, keepdims=True)
    ee = jnp.where(emask, jnp.exp(le - me), 0.0)
    pe = ee / jnp.sum(ee, axis=-1, keepdims=True)
    pe = jnp.where(emask, pe, -1.0)
    p1 = jnp.max(pe, axis=-1, keepdims=True)
    i1 = jnp.min(jnp.where(pe == p1, lane, big), axis=-1, keepdims=True)
    pe2 = jnp.where(lane == i1, -1.0, pe)
    p2 = jnp.max(pe2, axis=-1, keepdims=True)
    i2 = jnp.min(jnp.where(pe2 == p2, lane, big), axis=-1, keepdims=True)
    denom = p1 + p2
    gate1 = p_gsel * p1 / denom
    gate2 = p_gsel * p2 / denom
    e1 = i1 - N_EXPERT_GROUPS
    e2 = i2 - N_EXPERT_GROUPS
    sel1 = lane == i1
    sel2 = lane == i2
    sel = jnp.where(sel1 | sel2, 1.0, 0.0)
    before = jnp.dot(tri_ref[...], sel.astype(BF16), preferred_element_type=F32) + base
    r1 = jnp.sum(jnp.where(sel1, before, 0.0), axis=-1, keepdims=True)
    r2 = jnp.sum(jnp.where(sel2, before, 0.0), axis=-1, keepdims=True)
    route = jnp.where(lane == 0, e1, jnp.where(lane == 1, e2, jnp.where(lane == 2, gate1,
            jnp.where(lane == 3, gate2, jnp.where(lane == 4, r1, r2)))))
    route_ref[rows, :] = route
    return before[R - 1:R, :] + sel[R - 1:R, :]


def _outproj_router_call(mixes, w_out, x, g, b, rw, rb, alpha):
    T, D = x.shape
    W = mixes[0].shape[1]
    R = ROW_TILE
    rw_hi = rw.astype(BF16)
    rw_lo = (rw - rw_hi.astype(F32)).astype(BF16)
    rwa = jnp.concatenate([rw_hi, rw_lo], axis=1)
    idx = jnp.arange(ROUTE_CHUNK)
    tri = (idx[None, :] < idx[:, None]).astype(BF16)
    const = lambda shape: pl.BlockSpec(shape, lambda i: tuple(0 for _ in shape))
    return pl.pallas_call(
        functools.partial(_outproj_router_kernel, alpha=alpha),
        out_shape=[jax.ShapeDtypeStruct((T, D), F32), jax.ShapeDtypeStruct((T, ROUTE_LANES), F32),
                   jax.ShapeDtypeStruct((8, ROUTE_LANES), F32)],
        grid=(T // R,),
        in_specs=[pl.BlockSpec((R, W), lambda i: (i, 0))] * 4
        + [const((D, D)), pl.BlockSpec((R, D), lambda i: (i, 0)), const((1, D)), const((1, D)),
           const((D, 2 * ROUTE_LANES)), const((D, ROUTE_LANES)), const((1, ROUTE_LANES)),
           const((ROUTE_CHUNK, ROUTE_CHUNK))],
        out_specs=[pl.BlockSpec((R, D), lambda i: (i, 0)),
                   pl.BlockSpec((R, ROUTE_LANES), lambda i: (i, 0)),
                   const((8, ROUTE_LANES))],
        scratch_shapes=[pltpu.VMEM((1, ROUTE_LANES), F32)],
        compiler_params=_cparams(("arbitrary",), VMEM_LIMIT),
        name="outproj_router",
    )(*mixes, w_out, x, g.reshape(1, D), b.reshape(1, D), rwa, rw_hi, rb, tri)


IDX_SLOTS = 3
SUBLANES = 8


def _dispatch_kernel(fill_ref, dest_hbm, x_ref, xd_hbm, dsm, zbuf, sem_idx, sem_row, sem_fill):
    i = pl.program_id(0)
    n = pl.num_programs(0)
    R = x_ref.shape[0]
    RB = zbuf.shape[0]
    NI = TOP_K * R

    def idx_copy(t):
        s = t % 2
        return pltpu.make_async_copy(dest_hbm.at[t], dsm.at[pl.ds(s * NI, NI)], sem_idx.at[s])

    def fill_copy(blk):
        return pltpu.make_async_copy(zbuf, xd_hbm.at[pl.ds(pl.multiple_of(blk * RB, RB), RB), :], sem_fill)

    @pl.when(i == 0)
    def _():
        idx_copy(0).start()
        zbuf[...] = jnp.zeros(zbuf.shape, zbuf.dtype)

        def start_body(blk, _):
            @pl.when(fill_ref[blk] != 0)
            def _():
                fill_copy(blk).start()
            return 0

        def wait_body(blk, _):
            @pl.when(fill_ref[blk] != 0)
            def _():
                fill_copy(blk).wait()
            return 0

        lax.fori_loop(0, fill_ref.shape[0], start_body, 0)
        lax.fori_loop(0, fill_ref.shape[0], wait_body, 0)

    idx_copy(i).wait()

    @pl.when(i + 1 < n)
    def _():
        idx_copy(i + 1).start()

    base = (i % 2) * NI

    def body(grp, _):
        r0 = pl.multiple_of(grp * SUBLANES, SUBLANES)
        for u in range(SUBLANES):
            for k in range(TOP_K):
                d = dsm[base + TOP_K * (r0 + u) + k]
                pltpu.make_async_copy(x_ref.at[pl.ds(r0 + u, 1), :], xd_hbm.at[pl.ds(d, 1), :], sem_row).start()
        return 0

    lax.fori_loop(0, R // SUBLANES, body, 0)
    for _ in range(TOP_K):
        pltpu.make_async_copy(x_ref, x_ref, sem_row).wait()


def _dispatch_call(fill_flag, dest, x1, n_rows):
    T, D = x1.shape
    R = ROW_TILE
    grid_spec = pltpu.PrefetchScalarGridSpec(
        num_scalar_prefetch=1,
        grid=(T // R,),
        in_specs=[pl.BlockSpec(memory_space=pl.ANY),
                  pl.BlockSpec((R, D), lambda i, ff: (i, 0))],
        out_specs=pl.BlockSpec(memory_space=pl.ANY),
        scratch_shapes=[pltpu.SMEM((2 * TOP_K * R,), jnp.int32),
                        pltpu.VMEM((EXPERT_ROWS, D), F32),
                        pltpu.SemaphoreType.DMA((2,)),
                        pltpu.SemaphoreType.DMA,
                        pltpu.SemaphoreType.DMA],
    )
    return pl.pallas_call(
        _dispatch_kernel,
        out_shape=jax.ShapeDtypeStruct((n_rows, D), F32),
        grid_spec=grid_spec,
        compiler_params=_cparams(("arbitrary",), VMEM_LIMIT),
        name="moe_dispatch",
    )(fill_flag, dest, x1)


def _experts_kernel(bexp_ref, nact_ref, x_ref, w1_ref, w3_ref, w2_ref, y_ref):
    i = pl.program_id(0)

    @pl.when(i < nact_ref[0])
    def _():
        xb = x_ref[...].astype(BF16)
        h1 = jnp.dot(xb, w1_ref[0], preferred_element_type=F32)
        h3 = jnp.dot(xb, w3_ref[0], preferred_element_type=F32)
        h = (h1 * jax.nn.sigmoid(h1)) * h3
        y_ref[...] = jnp.dot(h.astype(BF16), w2_ref[0], preferred_element_type=F32)

    @pl.when(i >= nact_ref[0])
    def _():
        y_ref[...] = jnp.zeros(y_ref.shape, y_ref.dtype)


def _experts_call(block_expert, nact, xd, w1, w3, w2):
    P, D = xd.shape
    RB = EXPERT_ROWS
    DE = w1.shape[2]
    active = lambda i, be, na: (jnp.minimum(i, na[0] - 1), 0)
    grid_spec = pltpu.PrefetchScalarGridSpec(
        num_scalar_prefetch=2,
        grid=(P // RB,),
        in_specs=[pl.BlockSpec((RB, D), active),
                  pl.BlockSpec((1, D, DE), lambda i, be, na: (be[i], 0, 0)),
                  pl.BlockSpec((1, D, DE), lambda i, be, na: (be[i], 0, 0)),
                  pl.BlockSpec((1, DE, D), lambda i, be, na: (be[i], 0, 0))],
        out_specs=pl.BlockSpec((RB, D), lambda i, be, na: (i, 0)),
    )
    return pl.pallas_call(
        _experts_kernel,
        out_shape=jax.ShapeDtypeStruct((P, D), F32),
        grid_spec=grid_spec,
        compiler_params=_cparams(("arbitrary",), VMEM_LIMIT),
        name="experts",
    )(block_expert, nact, xd, w1, w3, w2)


def _combine_kernel(dest_hbm, y_hbm, route_ref, x_ref, g_ref, b_ref, o_ref,
                    dsm, ybuf, sem_idx, sem_row, *, alpha):
    i = pl.program_id(0)
    n = pl.num_programs(0)
    R, D = x_ref.shape
    NI = TOP_K * R

    def idx_copy(t):
        s = t % IDX_SLOTS
        return pltpu.make_async_copy(dest_hbm.at[t], dsm.at[pl.ds(s * NI, NI)], sem_idx.at[s])

    def issue_gather(t):
        base = (t % IDX_SLOTS) * NI
        s2 = t % 2

        def body(grp, _):
            r0 = pl.multiple_of(grp * SUBLANES, SUBLANES)
            for u in range(SUBLANES):
                for k in range(TOP_K):
                    d = dsm[base + TOP_K * (r0 + u) + k]
                    pltpu.make_async_copy(y_hbm.at[pl.ds(d, 1), :],
                                          ybuf.at[s2, pl.ds(k * R + r0 + u, 1), :], sem_row.at[s2]).start()
            return 0

        lax.fori_loop(0, R // SUBLANES, body, 0)

    @pl.when(i == 0)
    def _():
        c0 = idx_copy(0)
        c0.start()
        c0.wait()
        issue_gather(0)

        @pl.when(n > 1)
        def _():
            idx_copy(1).start()

    @pl.when(i + 1 < n)
    def _():
        idx_copy(i + 1).wait()
        issue_gather(i + 1)

    @pl.when(i + 2 < n)
    def _():
        idx_copy(i + 2).start()

    s2 = i % 2
    pltpu.make_async_copy(ybuf.at[s2], ybuf.at[s2], sem_row.at[s2]).wait()
    g0 = route_ref[:, 2:3]
    g1 = route_ref[:, 3:4]
    ffn = ybuf[s2, 0:R, :] * g0 + ybuf[s2, R:2 * R, :] * g1
    o_ref[...] = _layer_norm_rows(alpha * x_ref[...] + ffn, g_ref[...], b_ref[...])


def _combine_call(dest, y, route, x1, g, b, alpha):
    T, D = x1.shape
    R = ROW_TILE
    const = lambda shape: pl.BlockSpec(shape, lambda i: tuple(0 for _ in shape))
    return pl.pallas_call(
        functools.partial(_combine_kernel, alpha=alpha),
        out_shape=jax.ShapeDtypeStruct((T, D), F32),
        grid=(T // R,),
        in_specs=[pl.BlockSpec(memory_space=pl.ANY), pl.BlockSpec(memory_space=pl.ANY),
                  pl.BlockSpec((R, ROUTE_LANES), lambda i: (i, 0)),
                  pl.BlockSpec((R, D), lambda i: (i, 0)), const((1, D)), const((1, D))],
        out_specs=pl.BlockSpec((R, D), lambda i: (i, 0)),
        scratch_shapes=[pltpu.SMEM((IDX_SLOTS * TOP_K * R,), jnp.int32),
                        pltpu.VMEM((2, TOP_K * R, D), F32),
                        pltpu.SemaphoreType.DMA((IDX_SLOTS,)),
                        pltpu.SemaphoreType.DMA((2,))],
        compiler_params=_cparams(("arbitrary",), VMEM_LIMIT),
        name="combine_ln",
    )(dest, y, route, x1, g.reshape(1, D), b.reshape(1, D))


def _dispatch_tables(route, cnt, T):
    RB = EXPERT_ROWS
    A = T * TOP_K
    experts = jnp.arange(N_EXPERTS, dtype=jnp.int32)
    counts = cnt[0, N_EXPERT_GROUPS:N_EXPERT_GROUPS + N_EXPERTS].astype(jnp.int32)
    padded = (counts + RB - 1) // RB * RB
    p_ends = jnp.cumsum(padded)
    p_starts = p_ends - padded
    e = route[:, 0:TOP_K].astype(jnp.int32)
    rank = route[:, 4:4 + TOP_K].astype(jnp.int32)
    start_of = jnp.sum(jnp.where(e[:, :, None] == experts[None, None, :], p_starts[None, None, :], 0), axis=-1)
    dest = (start_of + rank).reshape(T // ROW_TILE, TOP_K * ROW_TILE)
    P = -(-A // RB) * RB + N_EXPERTS * RB
    n_blocks = P // RB
    blocks = jnp.arange(n_blocks, dtype=jnp.int32)
    block_expert = jnp.minimum(
        jnp.sum((p_ends[None, :] <= (blocks * RB)[:, None]).astype(jnp.int32), axis=1), N_EXPERTS - 1)
    nact = (p_ends[-1] // RB).astype(jnp.int32).reshape(1)
    last_partial = jnp.any((blocks[:, None] == (p_ends // RB - 1)[None, :]) & ((counts % RB) != 0)[None, :], axis=1)
    fill_flag = (last_partial | (blocks >= nact[0])).astype(jnp.int32)
    return dest, block_expert, nact, fill_flag, P


def kernel(x, ln_in_g, ln_in_b, w_in, b_f, conv_w, conv_b, sgu_ln_g, sgu_ln_b, sgu_w, sgu_b, grp_g, w_out, ln1_g, ln1_b, router_g_w, router_g_b, router_e_w, router_e_b, w1, w3, w2, ln2_g, ln2_b):
    B, S, D = x.shape
    depth = w_in.shape[0]
    W = D // N_MIXERS
    T = B * S
    alpha = (2.0 * depth) ** 0.25
    scale = 1.0 / math.sqrt(HEAD_DIM)

    xs = _layer_norm_call(x.reshape(T, D), ln_in_g, ln_in_b)
    for l in range(depth):
        wl = w_in[l]
        o_f = 6 * W
        o_cv = o_f + HEADS
        o_sg = o_cv + 3 * W
        f_cols = jnp.pad(wl[:, o_f:o_cv], ((0, 0), (0, LANES - HEADS)))
        w_all = jnp.concatenate(
            [wl[:, 0:W] * scale, wl[:, W:3 * W], wl[:, 3 * W:4 * W] * scale, wl[:, 4 * W:6 * W],
             wl[:, o_cv:o_sg], wl[:, o_sg:], f_cols], axis=1).astype(BF16)
        rw = jnp.pad(jnp.concatenate([router_g_w[l], router_e_w[l]], axis=1),
                     ((0, 0), (0, ROUTE_LANES - N_EXPERT_GROUPS - N_EXPERTS)))
        rb = jnp.pad(jnp.concatenate([router_g_b[l], router_e_b[l]]),
                     (0, ROUTE_LANES - N_EXPERT_GROUPS - N_EXPERTS)).reshape(1, ROUTE_LANES)
        gains = [grp_g[l, k * W:(k + 1) * W].reshape(1, W) for k in range(N_MIXERS)]

        qkv_a, qkv_b, cv, sg, f_pad = _inproj_call(xs, w_all, W)
        f_rows = f_pad[:, 0:HEADS].reshape(B, S, HEADS).transpose(0, 2, 1).reshape(B * HEADS, S)
        b_rows = jnp.tile(b_f[l], B).reshape(B * HEADS, 1)
        c = _fcum_call(f_rows, b_rows)
        c_tiles = c.reshape(B, HEADS * (S // ATT_K), ATT_K)
        mix_a = _sb_attn_call(qkv_a, gains[0], B, S, W)
        mix_b = _fox_attn_call(qkv_b, c_tiles, gains[1], B, S, W)
        mix_c, mix_d = _conv_sgu_call(cv, sg, conv_w[l], conv_b[l], sgu_ln_g[l], sgu_ln_b[l],
                                      sgu_w[l], sgu_b[l], gains[2][0], gains[3][0], S, W)
        x1, route, cnt = _outproj_router_call([mix_a, mix_b, mix_c, mix_d], w_out[l].astype(BF16), xs,
                                              ln1_g[l], ln1_b[l], rw, rb, alpha)
        dest, block_expert, nact, fill_flag, n_rows = _dispatch_tables(route, cnt, T)
        xd = _dispatch_call(fill_flag, dest, x1, n_rows)
        y = _experts_call(block_expert, nact, xd, w1[l].astype(BF16), w3[l].astype(BF16), w2[l].astype(BF16))
        xs = _combine_call(dest, y, route, x1, ln2_g[l], ln2_b[l], alpha)
    return xs.reshape(B, S, D)
```

```python
import functools
import math

import jax
import jax.numpy as jnp
from jax import lax
from jax.experimental import pallas as pl
from jax.experimental.pallas import tpu as pltpu

N_MIXERS = 4
HEAD_DIM = 64
HEADS = 4
CONV_WIDTH = 3
SGU_BLOCK = 128
CHUNK = 64
N_EXPERT_GROUPS = 4
EXPERTS_PER_GROUP = 8
N_EXPERTS = N_EXPERT_GROUPS * EXPERTS_PER_GROUP
TOP_K = 2
LN_EPS = 1e-5
RMS_EPS = 1e-6

LANES = 128
V7X_VMEM_BYTES = 64 * 1024 * 1024
VMEM_LIMIT = 48 * 1024 * 1024

ROW_TILE = 512
ATT_Q = 256
ATT_K = 256
EXPERT_ROWS = 256
ROUTE_LANES = LANES

F32 = jnp.float32
BF16 = jnp.bfloat16


def _cparams(sem, vmem=None):
    return pltpu.CompilerParams(dimension_semantics=sem, vmem_limit_bytes=vmem)


def _layer_norm_rows(y, g, b):
    mu = jnp.mean(y, axis=-1, keepdims=True)
    d = y - mu
    var = jnp.mean(d * d, axis=-1, keepdims=True)
    return d * lax.rsqrt(var + LN_EPS) * g + b


def _softplus(z):
    return jnp.maximum(z, 0.0) + jnp.log1p(jnp.exp(-jnp.abs(z)))


def _ln_kernel(x_ref, g_ref, b_ref, o_ref):
    o_ref[...] = _layer_norm_rows(x_ref[...], g_ref[...], b_ref[...])


def _layer_norm_call(x, g, b):
    T, D = x.shape
    return pl.pallas_call(
        _ln_kernel,
        out_shape=jax.ShapeDtypeStruct((T, D), F32),
        grid=(T // ROW_TILE,),
        in_specs=[pl.BlockSpec((ROW_TILE, D), lambda i: (i, 0)),
                  pl.BlockSpec((1, D), lambda i: (0, 0)),
                  pl.BlockSpec((1, D), lambda i: (0, 0))],
        out_specs=pl.BlockSpec((ROW_TILE, D), lambda i: (i, 0)),
        compiler_params=_cparams(("parallel",)),
        name="ln_in",
    )(x, g.reshape(1, D), b.reshape(1, D))


def _inproj_kernel(x_ref, w_ref, qa_ref, qb_ref, cv_ref, sg_ref, f_ref, *, widths):
    xb = x_ref[...].astype(BF16)
    off = 0
    for ref, wd in zip((qa_ref, qb_ref, cv_ref, sg_ref, f_ref), widths):
        ref[...] = jnp.dot(xb, w_ref[:, off:off + wd], preferred_element_type=F32).astype(ref.dtype)
        off += wd


def _inproj_call(x, w_all, W):
    T, D = x.shape
    widths = (3 * W, 3 * W, 3 * W, 2 * W, LANES)
    NC = sum(widths)
    dts = (BF16, BF16, F32, F32, F32)
    return pl.pallas_call(
        functools.partial(_inproj_kernel, widths=widths),
        out_shape=[jax.ShapeDtypeStruct((T, wd), dt) for wd, dt in zip(widths, dts)],
        grid=(T // ROW_TILE,),
        in_specs=[pl.BlockSpec((ROW_TILE, D), lambda i: (i, 0)),
                  pl.BlockSpec((D, NC), lambda i: (0, 0))],
        out_specs=[pl.BlockSpec((ROW_TILE, wd), lambda i: (i, 0)) for wd in widths],
        compiler_params=_cparams(("parallel",), VMEM_LIMIT),
        name="inproj",
    )(x, w_all)


def _fcum_kernel(f_ref, b_ref, tri_ref, c_ref):
    R, S = f_ref.shape
    y = f_ref[...] + b_ref[...]
    lf = -_softplus(-y)
    tri = tri_ref[...]
    carry = jnp.zeros((R, 1), F32)
    for blk in range(S // LANES):
        seg = lf[:, blk * LANES:(blk + 1) * LANES]
        s1 = seg.astype(BF16)
        r1 = seg - s1.astype(F32)
        s2 = r1.astype(BF16)
        s3 = (r1 - s2.astype(F32)).astype(BF16)
        cs = (jnp.dot(s1, tri, preferred_element_type=F32)
              + jnp.dot(s2, tri, preferred_element_type=F32)
              + jnp.dot(s3, tri, preferred_element_type=F32)) + carry
        c_ref[:, blk * LANES:(blk + 1) * LANES] = cs
        carry = cs[:, LANES - 1:LANES]


def _fcum_call(f_rows, b_rows):
    R, S = f_rows.shape
    idx = jnp.arange(LANES)
    tri = (idx[:, None] <= idx[None, :]).astype(BF16)
    return pl.pallas_call(
        _fcum_kernel,
        out_shape=jax.ShapeDtypeStruct((R, S), F32),
        name="forget_cumsum",
    )(f_rows, b_rows, tri)


def _rms_group_store(o_ref, out, g_ref):
    ms = jnp.mean(out * out, axis=-1, keepdims=True)
    o_ref[...] = (out * lax.rsqrt(ms + RMS_EPS) * g_ref[...]).astype(o_ref.dtype)


def _head_masks():
    lane = lax.broadcasted_iota(jnp.int32, (1, LANES), 1)
    return [lane < HEAD_DIM, lane >= HEAD_DIM]


def _nt_dot(a, b):
    return lax.dot_general(a, b, (((1,), (1,)), ((), ())), preferred_element_type=F32)


SB_DEAD = -104.0


def _neg_softplus(z):
    return jnp.minimum(-z, 0.0) - jnp.log(1.0 + jnp.exp(-jnp.abs(z)))


def _sb_attn_kernel(q_ref, k_ref, v_ref, m_ref, g_ref, o_ref):
    i = pl.program_id(1)
    Q = q_ref.shape[0]
    row = lax.broadcasted_iota(jnp.int32, (Q, ATT_K), 0)
    col = lax.broadcasted_iota(jnp.int32, (Q, ATT_K), 1)
    diag_mask = col < row
    hmasks = _head_masks()

    def tile(qms, j, state, mask):
        ks = pl.multiple_of(j * ATT_K, ATT_K)
        k_ts = [k_ref[pl.ds(ks, ATT_K), p * LANES:(p + 1) * LANES] for p in range(HEADS // 2)]
        v_ts = [v_ref[pl.ds(ks, ATT_K), p * LANES:(p + 1) * LANES] for p in range(HEADS // 2)]
        later_mat = m_ref[...]
        zs = [_nt_dot(qms[h], k_ts[h // 2]) for h in range(HEADS)]
        mid = []
        for h in range(HEADS):
            log_1m = _neg_softplus(zs[h])
            log_b = zs[h] + log_1m
            if mask is not None:
                log_1m = jnp.where(mask, log_1m, 0.0)
            hi = log_1m.astype(BF16)
            lo = (log_1m - hi.astype(F32)).astype(BF16)
            later = (jnp.dot(hi, later_mat, preferred_element_type=F32)
                     + jnp.dot(lo, later_mat, preferred_element_type=F32)) + state[2 * h]
            mid.append((log_b, later, log_1m[:, 0:1]))
        out = []
        for h in range(HEADS):
            log_b, later, first = mid[h]
            w = jnp.exp(log_b + later)
            if mask is not None:
                w = jnp.where(mask, w, 0.0)
            acc = state[2 * h + 1] + jnp.dot(w.astype(BF16), v_ts[h // 2], preferred_element_type=F32)
            out.extend((later[:, 0:1] + first, acc))
        return tuple(out)

    qms = []
    for h in range(HEADS):
        q_pair = q_ref[:, (h // 2) * LANES:(h // 2 + 1) * LANES]
        qms.append(jnp.where(hmasks[h % 2], q_pair, jnp.zeros_like(q_pair)))
    def alive(state):
        top = state[0]
        for h in range(1, HEADS):
            top = jnp.maximum(top, state[2 * h])
        return (jnp.max(top) >= SB_DEAD).astype(jnp.int32)

    state = (jnp.zeros((Q, 1), F32), jnp.zeros((Q, LANES), F32)) * HEADS
    state = tile(qms, i, state, diag_mask)

    def cond(c):
        return (c[0] <= i) & (c[1] > 0)

    def body(c):
        st = tile(qms, i - c[0], tuple(c[2:]), None)
        return (c[0] + 1, alive(st)) + st

    state = lax.while_loop(cond, body, (jnp.int32(1), alive(state)) + state)[2:]
    outs = [jnp.where(hmasks[0], state[4 * p + 1], state[4 * p + 3]) for p in range(HEADS // 2)]
    _rms_group_store(o_ref, jnp.concatenate(outs, axis=-1), g_ref)


def _fox_attn_kernel(q_ref, k_ref, v_ref, c_ref, g_ref, o_ref):
    i = pl.program_id(1)
    Q = q_ref.shape[0]
    nk = c_ref.shape[1] // HEADS
    row = lax.broadcasted_iota(jnp.int32, (Q, ATT_K), 0)
    col = lax.broadcasted_iota(jnp.int32, (Q, ATT_K), 1)
    diag_mask = col <= row
    hmasks = _head_masks()
    neg = jnp.finfo(F32).min

    def tile(qms, j, state, mask):
        ks = pl.multiple_of(j * ATT_K, ATT_K)
        k_ts = [k_ref[pl.ds(ks, ATT_K), p * LANES:(p + 1) * LANES] for p in range(HEADS // 2)]
        v_ts = [v_ref[pl.ds(ks, ATT_K), p * LANES:(p + 1) * LANES] for p in range(HEADS // 2)]
        ss = []
        for h in range(HEADS):
            c_row = c_ref[0, pl.ds(h * nk + j, 1), :]
            s = _nt_dot(qms[h], k_ts[h // 2]) - c_row
            if mask is not None:
                s = jnp.where(mask, s, neg)
            ss.append(s)
        mid = []
        for h in range(HEADS):
            m, l = state[3 * h], state[3 * h + 1]
            m_new = jnp.maximum(m, jnp.max(ss[h], axis=-1, keepdims=True))
            alpha = jnp.exp(m - m_new)
            p = jnp.exp(ss[h] - m_new)
            l = alpha * l + jnp.sum(p, axis=-1, keepdims=True)
            mid.append((m_new, l, alpha, p.astype(BF16)))
        out = []
        for h in range(HEADS):
            m_new, l, alpha, p = mid[h]
            acc = alpha * state[3 * h + 2] + jnp.dot(p, v_ts[h // 2], preferred_element_type=F32)
            out.extend((m_new, l, acc))
        return tuple(out)

    qms = []
    for h in range(HEADS):
        q_pair = q_ref[:, (h // 2) * LANES:(h // 2 + 1) * LANES]
        qms.append(jnp.where(hmasks[h % 2], q_pair, jnp.zeros_like(q_pair)))
    state = (jnp.full((Q, 1), neg, F32), jnp.zeros((Q, 1), F32), jnp.zeros((Q, LANES), F32)) * HEADS
    state = tile(qms, i, state, diag_mask)
    state = lax.fori_loop(1, i + 1, lambda jj, st: tile(qms, i - jj, st, None), state)
    heads = [state[3 * h + 2] / state[3 * h + 1] for h in range(HEADS)]
    outs = [jnp.where(hmasks[0], heads[2 * p], heads[2 * p + 1]) for p in range(HEADS // 2)]
    _rms_group_store(o_ref, jnp.concatenate(outs, axis=-1), g_ref)


def _attn_specs(B, S, W):
    nq = S // ATT_Q
    q_spec = pl.BlockSpec((ATT_Q, W), lambda b, i: (b * nq + i, 0))
    k_spec = pl.BlockSpec((S, W), lambda b, i: (b, 1))
    v_spec = pl.BlockSpec((S, W), lambda b, i: (b, 2))
    g_spec = pl.BlockSpec((1, W), lambda b, i: (0, 0))
    o_spec = pl.BlockSpec((ATT_Q, W), lambda b, i: (b * nq + i, 0))
    return nq, q_spec, k_spec, v_spec, g_spec, o_spec


def _sb_attn_call(qkv, gain, B, S, W):
    T = B * S
    nq, q_spec, k_spec, v_spec, g_spec, o_spec = _attn_specs(B, S, W)
    idx = jnp.arange(ATT_K)
    later_mat = (idx[:, None] > idx[None, :]).astype(BF16)
    return pl.pallas_call(
        _sb_attn_kernel,
        out_shape=jax.ShapeDtypeStruct((T, W), BF16),
        grid=(B, nq),
        in_specs=[q_spec, k_spec, v_spec,
                  pl.BlockSpec((ATT_K, ATT_K), lambda b, i: (0, 0)), g_spec],
        out_specs=o_spec,
        compiler_params=_cparams(("parallel", "arbitrary"), VMEM_LIMIT),
        name="sb_attn",
    )(qkv, qkv, qkv, later_mat, gain)


def _fox_attn_call(qkv, c_tiles, gain, B, S, W):
    T = B * S
    nq, q_spec, k_spec, v_spec, g_spec, o_spec = _attn_specs(B, S, W)
    rows = c_tiles.shape[1]
    return pl.pallas_call(
        _fox_attn_kernel,
        out_shape=jax.ShapeDtypeStruct((T, W), BF16),
        grid=(B, nq),
        in_specs=[q_spec, k_spec, v_spec,
                  pl.BlockSpec((1, rows, ATT_K), lambda b, i: (b, 0, 0)), g_spec],
        out_specs=o_spec,
        compiler_params=_cparams(("parallel", "arbitrary"), VMEM_LIMIT),
        name="fox_attn",
    )(qkv, qkv, qkv, c_tiles, gain)


HALO = 8


def _gelu_tanh(x):
    c = math.sqrt(2.0 / math.pi)
    return 0.5 * x * (1.0 + jnp.tanh(c * (x + 0.044715 * (x * x * x))))


def _conv_sgu_kernel(cv_ref, halo_ref, sg_ref, cw_ref, cb_ref, lg_ref, lb_ref, ws_ref, bs_ref,
                     gc_ref, gd_ref, oc_ref, od_ref, *, tiles_per_seq):
    R, W3 = cv_ref.shape
    W = W3 // 3
    i = pl.program_id(0)
    z = cv_ref[:, 2 * W:3 * W] * cv_ref[:, 0:W]
    zh = halo_ref[:, 2 * W:3 * W] * halo_ref[:, 0:W]
    zh = jnp.where(i % tiles_per_seq == 0, jnp.zeros_like(zh), zh)
    zf = jnp.concatenate([zh, z], axis=0)
    z1 = pltpu.roll(zf, 1, 0)[HALO:]
    z2 = pltpu.roll(zf, 2, 0)[HALO:]
    y = cb_ref[...] + cw_ref[0:1, :] * z2
    y = y + cw_ref[1:2, :] * z1
    y = y + cw_ref[2:3, :] * z
    _rms_group_store(oc_ref, cv_ref[:, W:2 * W] * y, gc_ref)
    gel = _gelu_tanh(sg_ref[...])
    u = gel[:, 0:W]
    vn = _layer_norm_rows(gel[:, W:2 * W], lg_ref[...], lb_ref[...]).astype(BF16)
    lane = lax.broadcasted_iota(jnp.int32, (1, W), 1)
    pi = lax.broadcasted_iota(jnp.int32, (SGU_BLOCK, SGU_BLOCK), 0) // CHUNK
    pj = lax.broadcasted_iota(jnp.int32, (SGU_BLOCK, SGU_BLOCK), 1) // CHUNK
    w_m = [jnp.where(pj <= pi, ws_ref[g], 0.0).astype(BF16) for g in range(HEADS)]
    blocks = []
    for n in range(R // SGU_BLOCK):
        vb = vn[n * SGU_BLOCK:(n + 1) * SGU_BLOCK, :]
        mixed = jnp.dot(w_m[0], vb, preferred_element_type=F32)
        for g in range(1, HEADS):
            mg = jnp.dot(w_m[g], vb, preferred_element_type=F32)
            mixed = jnp.where(lane >= g * HEAD_DIM, mg, mixed)
        blocks.append(mixed + bs_ref[...])
    mixed = jnp.concatenate(blocks, axis=0)
    _rms_group_store(od_ref, u * mixed, gd_ref)


def _conv_sgu_call(cv, sg, conv_w, conv_b, ln_g, ln_b, sgu_w, sgu_b, g_c, g_d, S, W):
    T = cv.shape[0]
    R = ROW_TILE
    tiles_per_seq = S // R
    bias_tile = jnp.repeat(sgu_b.T, HEAD_DIM, axis=1)
    row = lambda a: a.reshape(1, W)
    const = lambda shape: pl.BlockSpec(shape, lambda i: tuple(0 for _ in shape))
    return pl.pallas_call(
        functools.partial(_conv_sgu_kernel, tiles_per_seq=tiles_per_seq),
        out_shape=[jax.ShapeDtypeStruct((T, W), BF16)] * 2,
        grid=(T // R,),
        in_specs=[pl.BlockSpec((R, 3 * W), lambda i: (i, 0)),
                  pl.BlockSpec((HALO, 3 * W), lambda i: (jnp.maximum(i * (R // HALO) - 1, 0), 0)),
                  pl.BlockSpec((R, 2 * W), lambda i: (i, 0)),
                  const((CONV_WIDTH, W)), const((1, W)), const((1, W)), const((1, W)),
                  const((HEADS, SGU_BLOCK, SGU_BLOCK)), const((SGU_BLOCK, W)),
                  const((1, W)), const((1, W))],
        out_specs=[pl.BlockSpec((R, W), lambda i: (i, 0))] * 2,
        compiler_params=_cparams(("parallel",), VMEM_LIMIT),
        name="conv_sgu",
    )(cv, cv, sg, conv_w, row(conv_b), row(ln_g), row(ln_b), sgu_w, bias_tile, row(g_c), row(g_d))


ROUTE_CHUNK = ROW_TILE


def _outproj_router_kernel(ma_ref, mb_ref, mc_ref, md_ref, wo_ref, x_ref, g_ref, b_ref,
                           rwa_ref, rwh_ref, rb_ref, tri_ref, x1_ref, route_ref, cnt_ref, base_ref, *, alpha):
    @pl.when(pl.program_id(0) == 0)
    def _():
        base_ref[...] = jnp.zeros(base_ref.shape, base_ref.dtype)

    base = base_ref[...]
    for c in range(x_ref.shape[0] // ROUTE_CHUNK):
        rows = slice(c * ROUTE_CHUNK, (c + 1) * ROUTE_CHUNK)
        base = _outproj_route_rows(rows, ma_ref, mb_ref, mc_ref, md_ref, wo_ref, x_ref, g_ref, b_ref,
                                   rwa_ref, rwh_ref, rb_ref, tri_ref, x1_ref, route_ref, base, alpha)
    base_ref[...] = base
    cnt_ref[...] = jnp.broadcast_to(base, cnt_ref.shape)


def _outproj_route_rows(rows, ma_ref, mb_ref, mc_ref, md_ref, wo_ref, x_ref, g_ref, b_ref,
                        rwa_ref, rwh_ref, rb_ref, tri_ref, x1_ref, route_ref, base, alpha):
    W = ma_ref.shape[1]
    acc = jnp.dot(ma_ref[rows, :], wo_ref[0:W, :], preferred_element_type=F32)
    for k, ref in enumerate((mb_ref, mc_ref, md_ref), start=1):
        acc = acc + jnp.dot(ref[rows, :], wo_ref[k * W:(k + 1) * W, :], preferred_element_type=F32)
    x1 = _layer_norm_rows(alpha * x_ref[rows, :] + acc, g_ref[...], b_ref[...])
    x1_ref[rows, :] = x1
    xh = x1.astype(BF16)
    xl = (x1 - xh.astype(F32)).astype(BF16)
    both = jnp.dot(xh, rwa_ref[...], preferred_element_type=F32)
    logits = (both[:, 0:ROUTE_LANES] + both[:, ROUTE_LANES:2 * ROUTE_LANES]
              + jnp.dot(xl, rwh_ref[...], preferred_element_type=F32)) + rb_ref[...]
    R = logits.shape[0]
    lane = lax.broadcasted_iota(jnp.int32, (R, ROUTE_LANES), 1).astype(F32)
    big = float(ROUTE_LANES)
    ninf = jnp.finfo(F32).min
    gmask = lane < N_EXPERT_GROUPS
    lg = jnp.where(gmask, logits, ninf)
    mg = jnp.max(lg, axis=-1, keepdims=True)
    sum_g = jnp.sum(jnp.where(gmask, jnp.exp(lg - mg), 0.0), axis=-1, keepdims=True)
    p_gsel = 1.0 / sum_g
    g_sel = jnp.min(jnp.where(lg == mg, lane, big), axis=-1, keepdims=True)
    e_lo = N_EXPERT_GROUPS + EXPERTS_PER_GROUP * g_sel
    emask = (lane >= e_lo) & (lane < e_lo + EXPERTS_PER_GROUP)
    le = jnp.where(emask, logits, ninf)
    me = jnp.max(le, axis=-1, keepdims=True)
    ee = jnp.where(emask, jnp.exp(le - me), 0.0)
    pe = ee / jnp.sum(ee, axis=-1, keepdims=True)
    pe = jnp.where(emask, pe, -1.0)
    p1 = jnp.max(pe, axis=-1, keepdims=True)
    i1 = jnp.min(jnp.where(pe == p1, lane, big), axis=-1, keepdims=True)
    pe2 = jnp.where(lane == i1, -1.0, pe)
    p2 = jnp.max(pe2, axis=-1, keepdims=True)
    i2 = jnp.min(jnp.where(pe2 == p2, lane, big), axis=-1, keepdims=True)
    denom = p1 + p2
    gate1 = p_gsel * p1 / denom
    gate2 = p_gsel * p2 / denom
    e1 = i1 - N_EXPERT_GROUPS
    e2 = i2 - N_EXPERT_GROUPS
    sel1 = lane == i1
    sel2 = lane == i2
    sel = jnp.where(sel1 | sel2, 1.0, 0.0)
    before = jnp.dot(tri_ref[...], sel.astype(BF16), preferred_element_type=F32) + base
    r1 = jnp.sum(jnp.where(sel1, before, 0.0), axis=-1, keepdims=True)
    r2 = jnp.sum(jnp.where(sel2, before, 0.0), axis=-1, keepdims=True)
    route = jnp.where(lane == 0, e1, jnp.where(lane == 1, e2, jnp.where(lane == 2, gate1,
            jnp.where(lane == 3, gate2, jnp.where(lane == 4, r1, r2)))))
    route_ref[rows, :] = route
    return before[R - 1:R, :] + sel[R - 1:R, :]


def _outproj_router_call(mixes, w_out, x, g, b, rw, rb, alpha):
    T, D = x.shape
    W = mixes[0].shape[1]
    R = ROW_TILE
    rw_hi = rw.astype(BF16)
    rw_lo = (rw - rw_hi.astype(F32)).astype(BF16)
    rwa = jnp.concatenate([rw_hi, rw_lo], axis=1)
    idx = jnp.arange(ROUTE_CHUNK)
    tri = (idx[None, :] < idx[:, None]).astype(BF16)
    const = lambda shape: pl.BlockSpec(shape, lambda i: tuple(0 for _ in shape))
    return pl.pallas_call(
        functools.partial(_outproj_router_kernel, alpha=alpha),
        out_shape=[jax.ShapeDtypeStruct((T, D), F32), jax.ShapeDtypeStruct((T, ROUTE_LANES), F32),
                   jax.ShapeDtypeStruct((8, ROUTE_LANES), F32)],
        grid=(T // R,),
        in_specs=[pl.BlockSpec((R, W), lambda i: (i, 0))] * 4
        + [const((D, D)), pl.BlockSpec((R, D), lambda i: (i, 0)), const((1, D)), const((1, D)),
           const((D, 2 * ROUTE_LANES)), const((D, ROUTE_LANES)), const((1, ROUTE_LANES)),
           const((ROUTE_CHUNK, ROUTE_CHUNK))],
        out_specs=[pl.BlockSpec((R, D), lambda i: (i, 0)),
                   pl.BlockSpec((R, ROUTE_LANES), lambda i: (i, 0)),
                   const((8, ROUTE_LANES))],
        scratch_shapes=[pltpu.VMEM((1, ROUTE_LANES), F32)],
        compiler_params=_cparams(("arbitrary",), VMEM_LIMIT),
        name="outproj_router",
    )(*mixes, w_out, x, g.reshape(1, D), b.reshape(1, D), rwa, rw_hi, rb, tri)


IDX_SLOTS = 3
SUBLANES = 8


def _dispatch_kernel(fill_ref, dest_hbm, x_ref, xd_hbm, dsm, zbuf, sem_idx, sem_row, sem_fill):
    i = pl.program_id(0)
    n = pl.num_programs(0)
    R = x_ref.shape[0]
    RB = zbuf.shape[0]
    NI = TOP_K * R

    def idx_copy(t):
        s = t % 2
        return pltpu.make_async_copy(dest_hbm.at[t], dsm.at[pl.ds(s * NI, NI)], sem_idx.at[s])

    def fill_copy(blk):
        return pltpu.make_async_copy(zbuf, xd_hbm.at[pl.ds(pl.multiple_of(blk * RB, RB), RB), :], sem_fill)

    @pl.when(i == 0)
    def _():
        idx_copy(0).start()
        zbuf[...] = jnp.zeros(zbuf.shape, zbuf.dtype)

        def start_body(blk, _):
            @pl.when(fill_ref[blk] != 0)
            def _():
                fill_copy(blk).start()
            return 0

        def wait_body(blk, _):
            @pl.when(fill_ref[blk] != 0)
            def _():
                fill_copy(blk).wait()
            return 0

        lax.fori_loop(0, fill_ref.shape[0], start_body, 0)
        lax.fori_loop(0, fill_ref.shape[0], wait_body, 0)

    idx_copy(i).wait()

    @pl.when(i + 1 < n)
    def _():
        idx_copy(i + 1).start()

    base = (i % 2) * NI

    def body(grp, _):
        r0 = pl.multiple_of(grp * SUBLANES, SUBLANES)
        for u in range(SUBLANES):
            for k in range(TOP_K):
                d = dsm[base + TOP_K * (r0 + u) + k]
                pltpu.make_async_copy(x_ref.at[pl.ds(r0 + u, 1), :], xd_hbm.at[pl.ds(d, 1), :], sem_row).start()
        return 0

    lax.fori_loop(0, R // SUBLANES, body, 0)
    for _ in range(TOP_K):
        pltpu.make_async_copy(x_ref, x_ref, sem_row).wait()


def _dispatch_call(fill_flag, dest, x1, n_rows):
    T, D = x1.shape
    R = ROW_TILE
    grid_spec = pltpu.PrefetchScalarGridSpec(
        num_scalar_prefetch=1,
        grid=(T // R,),
        in_specs=[pl.BlockSpec(memory_space=pl.ANY),
                  pl.BlockSpec((R, D), lambda i, ff: (i, 0))],
        out_specs=pl.BlockSpec(memory_space=pl.ANY),
        scratch_shapes=[pltpu.SMEM((2 * TOP_K * R,), jnp.int32),
                        pltpu.VMEM((EXPERT_ROWS, D), F32),
                        pltpu.SemaphoreType.DMA((2,)),
                        pltpu.SemaphoreType.DMA,
                        pltpu.SemaphoreType.DMA],
    )
    return pl.pallas_call(
        _dispatch_kernel,
        out_shape=jax.ShapeDtypeStruct((n_rows, D), F32),
        grid_spec=grid_spec,
        compiler_params=_cparams(("arbitrary",), VMEM_LIMIT),
        name="moe_dispatch",
    )(fill_flag, dest, x1)


def _experts_kernel(bexp_ref, nact_ref, x_ref, w1_ref, w3_ref, w2_ref, y_ref):
    i = pl.program_id(0)

    @pl.when(i < nact_ref[0])
    def _():
        xb = x_ref[...].astype(BF16)
        h1 = jnp.dot(xb, w1_ref[0].astype(BF16), preferred_element_type=F32)
        h3 = jnp.dot(xb, w3_ref[0].astype(BF16), preferred_element_type=F32)
        h = (h1 * jax.nn.sigmoid(h1)) * h3
        y_ref[...] = jnp.dot(h.astype(BF16), w2_ref[0].astype(BF16), preferred_element_type=F32)

    @pl.when(i >= nact_ref[0])
    def _():
        y_ref[...] = jnp.zeros(y_ref.shape, y_ref.dtype)


def _experts_call(block_expert, nact, xd, w1, w3, w2):
    P, D = xd.shape
    RB = EXPERT_ROWS
    DE = w1.shape[2]
    active = lambda i, be, na: (jnp.minimum(i, na[0] - 1), 0)
    grid_spec = pltpu.PrefetchScalarGridSpec(
        num_scalar_prefetch=2,
        grid=(P // RB,),
        in_specs=[pl.BlockSpec((RB, D), active),
                  pl.BlockSpec((1, D, DE), lambda i, be, na: (be[i], 0, 0)),
                  pl.BlockSpec((1, D, DE), lambda i, be, na: (be[i], 0, 0)),
                  pl.BlockSpec((1, DE, D), lambda i, be, na: (be[i], 0, 0))],
        out_specs=pl.BlockSpec((RB, D), lambda i, be, na: (i, 0)),
    )
    return pl.pallas_call(
        _experts_kernel,
        out_shape=jax.ShapeDtypeStruct((P, D), F32),
        grid_spec=grid_spec,
        compiler_params=_cparams(("arbitrary",), VMEM_LIMIT),
        name="experts",
    )(block_expert, nact, xd, w1, w3, w2)


def _combine_kernel(dest_hbm, y_hbm, route_ref, x_ref, g_ref, b_ref, o_ref,
                    dsm, ybuf, sem_idx, sem_row, *, alpha):
    i = pl.program_id(0)
    n = pl.num_programs(0)
    R, D = x_ref.shape
    NI = TOP_K * R

    def idx_copy(t):
        s = t % IDX_SLOTS
        return pltpu.make_async_copy(dest_hbm.at[t], dsm.at[pl.ds(s * NI, NI)], sem_idx.at[s])

    def issue_gather(t):
        base = (t % IDX_SLOTS) * NI
        s2 = t % 2

        def body(grp, _):
            r0 = pl.multiple_of(grp * SUBLANES, SUBLANES)
            for u in range(SUBLANES):
                for k in range(TOP_K):
                    d = dsm[base + TOP_K * (r0 + u) + k]
                    pltpu.make_async_copy(y_hbm.at[pl.ds(d, 1), :],
                                          ybuf.at[s2, pl.ds(k * R + r0 + u, 1), :], sem_row.at[s2]).start()
            return 0

        lax.fori_loop(0, R // SUBLANES, body, 0)

    @pl.when(i == 0)
    def _():
        c0 = idx_copy(0)
        c0.start()
        c0.wait()
        issue_gather(0)

        @pl.when(n > 1)
        def _():
            idx_copy(1).start()

    @pl.when(i + 1 < n)
    def _():
        idx_copy(i + 1).wait()
        issue_gather(i + 1)

    @pl.when(i + 2 < n)
    def _():
        idx_copy(i + 2).start()

    s2 = i % 2
    pltpu.make_async_copy(ybuf.at[s2], ybuf.at[s2], sem_row.at[s2]).wait()
    g0 = route_ref[:, 2:3]
    g1 = route_ref[:, 3:4]
    ffn = ybuf[s2, 0:R, :] * g0 + ybuf[s2, R:2 * R, :] * g1
    o_ref[...] = _layer_norm_rows(alpha * x_ref[...] + ffn, g_ref[...], b_ref[...])


def _combine_call(dest, y, route, x1, g, b, alpha):
    T, D = x1.shape
    R = ROW_TILE
    const = lambda shape: pl.BlockSpec(shape, lambda i: tuple(0 for _ in shape))
    return pl.pallas_call(
        functools.partial(_combine_kernel, alpha=alpha),
        out_shape=jax.ShapeDtypeStruct((T, D), F32),
        grid=(T // R,),
        in_specs=[pl.BlockSpec(memory_space=pl.ANY), pl.BlockSpec(memory_space=pl.ANY),
                  pl.BlockSpec((R, ROUTE_LANES), lambda i: (i, 0)),
                  pl.BlockSpec((R, D), lambda i: (i, 0)), const((1, D)), const((1, D))],
        out_specs=pl.BlockSpec((R, D), lambda i: (i, 0)),
        scratch_shapes=[pltpu.SMEM((IDX_SLOTS * TOP_K * R,), jnp.int32),
                        pltpu.VMEM((2, TOP_K * R, D), F32),
                        pltpu.SemaphoreType.DMA((IDX_SLOTS,)),
                        pltpu.SemaphoreType.DMA((2,))],
        compiler_params=_cparams(("arbitrary",), VMEM_LIMIT),
        name="combine_ln",
    )(dest, y, route, x1, g.reshape(1, D), b.reshape(1, D))


def _dispatch_tables(route, cnt, T):
    RB = EXPERT_ROWS
    A = T * TOP_K
    experts = jnp.arange(N_EXPERTS, dtype=jnp.int32)
    counts = cnt[0, N_EXPERT_GROUPS:N_EXPERT_GROUPS + N_EXPERTS].astype(jnp.int32)
    padded = (counts + RB - 1) // RB * RB
    p_ends = jnp.cumsum(padded)
    p_starts = p_ends - padded
    e = route[:, 0:TOP_K].astype(jnp.int32)
    rank = route[:, 4:4 + TOP_K].astype(jnp.int32)
    start_of = jnp.sum(jnp.where(e[:, :, None] == experts[None, None, :], p_starts[None, None, :], 0), axis=-1)
    dest = (start_of + rank).reshape(T // ROW_TILE, TOP_K * ROW_TILE)
    P = -(-A // RB) * RB + N_EXPERTS * RB
    n_blocks = P // RB
    blocks = jnp.arange(n_blocks, dtype=jnp.int32)
    block_expert = jnp.minimum(
        jnp.sum((p_ends[None, :] <= (blocks * RB)[:, None]).astype(jnp.int32), axis=1), N_EXPERTS - 1)
    nact = (p_ends[-1] // RB).astype(jnp.int32).reshape(1)
    last_partial = jnp.any((blocks[:, None] == (p_ends // RB - 1)[None, :]) & ((counts % RB) != 0)[None, :], axis=1)
    fill_flag = (last_partial | (blocks >= nact[0])).astype(jnp.int32)
    return dest, block_expert, nact, fill_flag, P


def kernel(x, ln_in_g, ln_in_b, w_in, b_f, conv_w, conv_b, sgu_ln_g, sgu_ln_b, sgu_w, sgu_b, grp_g, w_out, ln1_g, ln1_b, router_g_w, router_g_b, router_e_w, router_e_b, w1, w3, w2, ln2_g, ln2_b):
    B, S, D = x.shape
    depth = w_in.shape[0]
    W = D // N_MIXERS
    T = B * S
    alpha = (2.0 * depth) ** 0.25
    scale = 1.0 / math.sqrt(HEAD_DIM)

    xs = _layer_norm_call(x.reshape(T, D), ln_in_g, ln_in_b)
    for l in range(depth):
        wl = w_in[l]
        o_f = 6 * W
        o_cv = o_f + HEADS
        o_sg = o_cv + 3 * W
        f_cols = jnp.pad(wl[:, o_f:o_cv], ((0, 0), (0, LANES - HEADS)))
        w_all = jnp.concatenate(
            [wl[:, 0:W] * scale, wl[:, W:3 * W], wl[:, 3 * W:4 * W] * scale, wl[:, 4 * W:6 * W],
             wl[:, o_cv:o_sg], wl[:, o_sg:], f_cols], axis=1).astype(BF16)
        rw = jnp.pad(jnp.concatenate([router_g_w[l], router_e_w[l]], axis=1),
                     ((0, 0), (0, ROUTE_LANES - N_EXPERT_GROUPS - N_EXPERTS)))
        rb = jnp.pad(jnp.concatenate([router_g_b[l], router_e_b[l]]),
                     (0, ROUTE_LANES - N_EXPERT_GROUPS - N_EXPERTS)).reshape(1, ROUTE_LANES)
        gains = [grp_g[l, k * W:(k + 1) * W].reshape(1, W) for k in range(N_MIXERS)]

        qkv_a, qkv_b, cv, sg, f_pad = _inproj_call(xs, w_all, W)
        f_rows = f_pad[:, 0:HEADS].reshape(B, S, HEADS).transpose(0, 2, 1).reshape(B * HEADS, S)
        b_rows = jnp.tile(b_f[l], B).reshape(B * HEADS, 1)
        c = _fcum_call(f_rows, b_rows)
        c_tiles = c.reshape(B, HEADS * (S // ATT_K), ATT_K)
        mix_a = _sb_attn_call(qkv_a, gains[0], B, S, W)
        mix_b = _fox_attn_call(qkv_b, c_tiles, gains[1], B, S, W)
        mix_c, mix_d = _conv_sgu_call(cv, sg, conv_w[l], conv_b[l], sgu_ln_g[l], sgu_ln_b[l],
                                      sgu_w[l], sgu_b[l], gains[2][0], gains[3][0], S, W)
        x1, route, cnt = _outproj_router_call([mix_a, mix_b, mix_c, mix_d], w_out[l].astype(BF16), xs,
                                              ln1_g[l], ln1_b[l], rw, rb, alpha)
        dest, block_expert, nact, fill_flag, n_rows = _dispatch_tables(route, cnt, T)
        xd = _dispatch_call(fill_flag, dest, x1, n_rows)
        y = _experts_call(block_expert, nact, xd, w1[l], w3[l], w2[l])
        xs = _combine_call(dest, y, route, x1, ln2_g[l], ln2_b[l], alpha)
    return xs.reshape(B, S, D)
```

```python
import functools
import math

import jax
import jax.numpy as jnp
from jax import lax
from jax.experimental import pallas as pl
from jax.experimental.pallas import tpu as pltpu

N_MIXERS = 4
HEAD_DIM = 64
HEADS = 4
CONV_WIDTH = 3
SGU_BLOCK = 128
CHUNK = 64
N_EXPERT_GROUPS = 4
EXPERTS_PER_GROUP = 8
N_EXPERTS = N_EXPERT_GROUPS * EXPERTS_PER_GROUP
TOP_K = 2
LN_EPS = 1e-5
RMS_EPS = 1e-6

LANES = 128
V7X_VMEM_BYTES = 64 * 1024 * 1024
VMEM_LIMIT = 48 * 1024 * 1024

ROW_TILE = 512
ATT_Q = 256
ATT_K = 256
EXPERT_ROWS = 128
ROUTE_LANES = LANES

F32 = jnp.float32
BF16 = jnp.bfloat16


def _cparams(sem, vmem=None):
    return pltpu.CompilerParams(dimension_semantics=sem, vmem_limit_bytes=vmem)


def _layer_norm_rows(y, g, b):
    mu = jnp.mean(y, axis=-1, keepdims=True)
    d = y - mu
    var = jnp.mean(d * d, axis=-1, keepdims=True)
    return d * lax.rsqrt(var + LN_EPS) * g + b


def _softplus(z):
    return jnp.maximum(z, 0.0) + jnp.log1p(jnp.exp(-jnp.abs(z)))


def _inproj_kernel(x_ref, w_ref, *refs, widths, entry_norm):
    if entry_norm:
        g_ref, b_ref, qa_ref, qb_ref, cv_ref, sg_ref, f_ref, xn_ref = refs
        x = _layer_norm_rows(x_ref[...], g_ref[...], b_ref[...])
        xn_ref[...] = x
    else:
        qa_ref, qb_ref, cv_ref, sg_ref, f_ref = refs
        x = x_ref[...]
    xb = x.astype(BF16)
    off = 0
    for ref, wd in zip((qa_ref, qb_ref, cv_ref, sg_ref, f_ref), widths):
        ref[...] = jnp.dot(xb, w_ref[:, off:off + wd], preferred_element_type=F32).astype(ref.dtype)
        off += wd


def _inproj_call(x, w_all, W, entry_ln=None):
    T, D = x.shape
    widths = (3 * W, 3 * W, 3 * W, 2 * W, LANES)
    NC = sum(widths)
    dts = (BF16, BF16, F32, F32, F32)
    row_spec = pl.BlockSpec((ROW_TILE, D), lambda i: (i, 0))
    vec_spec = pl.BlockSpec((1, D), lambda i: (0, 0))
    in_specs = [row_spec, pl.BlockSpec((D, NC), lambda i: (0, 0))]
    out_shape = [jax.ShapeDtypeStruct((T, wd), dt) for wd, dt in zip(widths, dts)]
    out_specs = [pl.BlockSpec((ROW_TILE, wd), lambda i: (i, 0)) for wd in widths]
    args = [x, w_all]
    if entry_ln is not None:
        in_specs += [vec_spec, vec_spec]
        out_shape.append(jax.ShapeDtypeStruct((T, D), F32))
        out_specs.append(row_spec)
        args += [entry_ln[0].reshape(1, D), entry_ln[1].reshape(1, D)]
    return pl.pallas_call(
        functools.partial(_inproj_kernel, widths=widths, entry_norm=entry_ln is not None),
        out_shape=out_shape,
        grid=(T // ROW_TILE,),
        in_specs=in_specs,
        out_specs=out_specs,
        compiler_params=_cparams(("parallel",), VMEM_LIMIT),
        name="inproj",
    )(*args)


def _fcum_kernel(f_ref, b_ref, tri_ref, c_ref):
    R, S = f_ref.shape
    y = f_ref[...] + b_ref[...]
    lf = -_softplus(-y)
    tri = tri_ref[...]
    carry = jnp.zeros((R, 1), F32)
    for blk in range(S // LANES):
        seg = lf[:, blk * LANES:(blk + 1) * LANES]
        s1 = seg.astype(BF16)
        r1 = seg - s1.astype(F32)
        s2 = r1.astype(BF16)
        s3 = (r1 - s2.astype(F32)).astype(BF16)
        cs = (jnp.dot(s1, tri, preferred_element_type=F32)
              + jnp.dot(s2, tri, preferred_element_type=F32)
              + jnp.dot(s3, tri, preferred_element_type=F32)) + carry
        c_ref[:, blk * LANES:(blk + 1) * LANES] = cs
        carry = cs[:, LANES - 1:LANES]


def _fcum_call(f_rows, b_rows):
    R, S = f_rows.shape
    idx = jnp.arange(LANES)
    tri = (idx[:, None] <= idx[None, :]).astype(BF16)
    return pl.pallas_call(
        _fcum_kernel,
        out_shape=jax.ShapeDtypeStruct((R, S), F32),
        name="forget_cumsum",
    )(f_rows, b_rows, tri)


def _rms_group_store(o_ref, out, g_ref):
    ms = jnp.mean(out * out, axis=-1, keepdims=True)
    o_ref[...] = (out * lax.rsqrt(ms + RMS_EPS) * g_ref[...]).astype(o_ref.dtype)


def _head_masks():
    lane = lax.broadcasted_iota(jnp.int32, (1, LANES), 1)
    return [lane < HEAD_DIM, lane >= HEAD_DIM]


def _nt_dot(a, b):
    return lax.dot_general(a, b, (((1,), (1,)), ((), ())), preferred_element_type=F32)


SB_DEAD = -104.0


def _neg_softplus(z):
    return jnp.minimum(-z, 0.0) - jnp.log(1.0 + jnp.exp(-jnp.abs(z)))


def _sb_attn_kernel(q_ref, k_ref, v_ref, m_ref, g_ref, o_ref):
    i = pl.program_id(1)
    Q = q_ref.shape[0]
    row = lax.broadcasted_iota(jnp.int32, (Q, ATT_K), 0)
    col = lax.broadcasted_iota(jnp.int32, (Q, ATT_K), 1)
    diag_mask = col < row
    hmasks = _head_masks()

    def tile(qms, j, state, mask):
        ks = pl.multiple_of(j * ATT_K, ATT_K)
        k_ts = [k_ref[pl.ds(ks, ATT_K), p * LANES:(p + 1) * LANES] for p in range(HEADS // 2)]
        v_ts = [v_ref[pl.ds(ks, ATT_K), p * LANES:(p + 1) * LANES] for p in range(HEADS // 2)]
        later_mat = m_ref[...]
        zs = [_nt_dot(qms[h], k_ts[h // 2]) for h in range(HEADS)]
        mid = []
        for h in range(HEADS):
            log_1m = _neg_softplus(zs[h])
            log_b = zs[h] + log_1m
            if mask is not None:
                log_1m = jnp.where(mask, log_1m, 0.0)
            hi = log_1m.astype(BF16)
            lo = (log_1m - hi.astype(F32)).astype(BF16)
            later = (jnp.dot(hi, later_mat, preferred_element_type=F32)
                     + jnp.dot(lo, later_mat, preferred_element_type=F32)) + state[2 * h]
            mid.append((log_b, later, log_1m[:, 0:1]))
        out = []
        for h in range(HEADS):
            log_b, later, first = mid[h]
            w = jnp.exp(log_b + later)
            if mask is not None:
                w = jnp.where(mask, w, 0.0)
            acc = state[2 * h + 1] + jnp.dot(w.astype(BF16), v_ts[h // 2], preferred_element_type=F32)
            out.extend((later[:, 0:1] + first, acc))
        return tuple(out)

    qms = []
    for h in range(HEADS):
        q_pair = q_ref[:, (h // 2) * LANES:(h // 2 + 1) * LANES]
        qms.append(jnp.where(hmasks[h % 2], q_pair, jnp.zeros_like(q_pair)))
    def alive(state):
        top = state[0]
        for h in range(1, HEADS):
            top = jnp.maximum(top, state[2 * h])
        return (jnp.max(top) >= SB_DEAD).astype(jnp.int32)

    state = (jnp.zeros((Q, 1), F32), jnp.zeros((Q, LANES), F32)) * HEADS
    state = tile(qms, i, state, diag_mask)

    def cond(c):
        return (c[0] <= i) & (c[1] > 0)

    def body(c):
        st = tile(qms, i - c[0], tuple(c[2:]), None)
        return (c[0] + 1, alive(st)) + st

    state = lax.while_loop(cond, body, (jnp.int32(1), alive(state)) + state)[2:]
    outs = [jnp.where(hmasks[0], state[4 * p + 1], state[4 * p + 3]) for p in range(HEADS // 2)]
    _rms_group_store(o_ref, jnp.concatenate(outs, axis=-1), g_ref)


def _fox_attn_kernel(q_ref, k_ref, v_ref, c_ref, g_ref, o_ref):
    i = pl.program_id(1)
    Q = q_ref.shape[0]
    nk = c_ref.shape[1] // HEADS
    row = lax.broadcasted_iota(jnp.int32, (Q, ATT_K), 0)
    col = lax.broadcasted_iota(jnp.int32, (Q, ATT_K), 1)
    diag_mask = col <= row
    hmasks = _head_masks()
    neg = jnp.finfo(F32).min

    def tile(qms, j, state, mask):
        ks = pl.multiple_of(j * ATT_K, ATT_K)
        k_ts = [k_ref[pl.ds(ks, ATT_K), p * LANES:(p + 1) * LANES] for p in range(HEADS // 2)]
        v_ts = [v_ref[pl.ds(ks, ATT_K), p * LANES:(p + 1) * LANES] for p in range(HEADS // 2)]
        ss = []
        for h in range(HEADS):
            c_row = c_ref[0, pl.ds(h * nk + j, 1), :]
            s = _nt_dot(qms[h], k_ts[h // 2]) - c_row
            if mask is not None:
                s = jnp.where(mask, s, neg)
            ss.append(s)
        mid = []
        for h in range(HEADS):
            m, l = state[3 * h], state[3 * h + 1]
            m_new = jnp.maximum(m, jnp.max(ss[h], axis=-1, keepdims=True))
            alpha = jnp.exp(m - m_new)
            p = jnp.exp(ss[h] - m_new)
            l = alpha * l + jnp.sum(p, axis=-1, keepdims=True)
            mid.append((m_new, l, alpha, p.astype(BF16)))
        out = []
        for h in range(HEADS):
            m_new, l, alpha, p = mid[h]
            acc = alpha * state[3 * h + 2] + jnp.dot(p, v_ts[h // 2], preferred_element_type=F32)
            out.extend((m_new, l, acc))
        return tuple(out)

    qms = []
    for h in range(HEADS):
        q_pair = q_ref[:, (h // 2) * LANES:(h // 2 + 1) * LANES]
        qms.append(jnp.where(hmasks[h % 2], q_pair, jnp.zeros_like(q_pair)))
    state = (jnp.full((Q, 1), neg, F32), jnp.zeros((Q, 1), F32), jnp.zeros((Q, LANES), F32)) * HEADS
    state = tile(qms, i, state, diag_mask)
    state = lax.fori_loop(1, i + 1, lambda jj, st: tile(qms, i - jj, st, None), state)
    heads = [state[3 * h + 2] / state[3 * h + 1] for h in range(HEADS)]
    outs = [jnp.where(hmasks[0], heads[2 * p], heads[2 * p + 1]) for p in range(HEADS // 2)]
    _rms_group_store(o_ref, jnp.concatenate(outs, axis=-1), g_ref)


def _attn_specs(B, S, W):
    nq = S // ATT_Q
    q_spec = pl.BlockSpec((ATT_Q, W), lambda b, i: (b * nq + i, 0))
    k_spec = pl.BlockSpec((S, W), lambda b, i: (b, 1))
    v_spec = pl.BlockSpec((S, W), lambda b, i: (b, 2))
    g_spec = pl.BlockSpec((1, W), lambda b, i: (0, 0))
    o_spec = pl.BlockSpec((ATT_Q, W), lambda b, i: (b * nq + i, 0))
    return nq, q_spec, k_spec, v_spec, g_spec, o_spec


def _sb_attn_call(qkv, gain, B, S, W):
    T = B * S
    nq, q_spec, k_spec, v_spec, g_spec, o_spec = _attn_specs(B, S, W)
    idx = jnp.arange(ATT_K)
    later_mat = (idx[:, None] > idx[None, :]).astype(BF16)
    return pl.pallas_call(
        _sb_attn_kernel,
        out_shape=jax.ShapeDtypeStruct((T, W), BF16),
        grid=(B, nq),
        in_specs=[q_spec, k_spec, v_spec,
                  pl.BlockSpec((ATT_K, ATT_K), lambda b, i: (0, 0)), g_spec],
        out_specs=o_spec,
        compiler_params=_cparams(("parallel", "arbitrary"), VMEM_LIMIT),
        name="sb_attn",
    )(qkv, qkv, qkv, later_mat, gain)


def _fox_attn_call(qkv, c_tiles, gain, B, S, W):
    T = B * S
    nq, q_spec, k_spec, v_spec, g_spec, o_spec = _attn_specs(B, S, W)
    rows = c_tiles.shape[1]
    return pl.pallas_call(
        _fox_attn_kernel,
        out_shape=jax.ShapeDtypeStruct((T, W), BF16),
        grid=(B, nq),
        in_specs=[q_spec, k_spec, v_spec,
                  pl.BlockSpec((1, rows, ATT_K), lambda b, i: (b, 0, 0)), g_spec],
        out_specs=o_spec,
        compiler_params=_cparams(("parallel", "arbitrary"), VMEM_LIMIT),
        name="fox_attn",
    )(qkv, qkv, qkv, c_tiles, gain)


HALO = 8


def _gelu_tanh(x):
    c = math.sqrt(2.0 / math.pi)
    return 0.5 * x * (1.0 + jnp.tanh(c * (x + 0.044715 * (x * x * x))))


def _conv_sgu_kernel(cv_ref, halo_ref, sg_ref, cw_ref, cb_ref, lg_ref, lb_ref, ws_ref, bs_ref,
                     gc_ref, gd_ref, oc_ref, od_ref, *, tiles_per_seq):
    R, W3 = cv_ref.shape
    W = W3 // 3
    i = pl.program_id(0)
    z = cv_ref[:, 2 * W:3 * W] * cv_ref[:, 0:W]
    zh = halo_ref[:, 2 * W:3 * W] * halo_ref[:, 0:W]
    zh = jnp.where(i % tiles_per_seq == 0, jnp.zeros_like(zh), zh)
    zf = jnp.concatenate([zh, z], axis=0)
    z1 = pltpu.roll(zf, 1, 0)[HALO:]
    z2 = pltpu.roll(zf, 2, 0)[HALO:]
    y = cb_ref[...] + cw_ref[0:1, :] * z2
    y = y + cw_ref[1:2, :] * z1
    y = y + cw_ref[2:3, :] * z
    _rms_group_store(oc_ref, cv_ref[:, W:2 * W] * y, gc_ref)
    gel = _gelu_tanh(sg_ref[...])
    u = gel[:, 0:W]
    vn = _layer_norm_rows(gel[:, W:2 * W], lg_ref[...], lb_ref[...]).astype(BF16)
    lane = lax.broadcasted_iota(jnp.int32, (1, W), 1)
    pi = lax.broadcasted_iota(jnp.int32, (SGU_BLOCK, SGU_BLOCK), 0) // CHUNK
    pj = lax.broadcasted_iota(jnp.int32, (SGU_BLOCK, SGU_BLOCK), 1) // CHUNK
    w_m = [jnp.where(pj <= pi, ws_ref[g], 0.0).astype(BF16) for g in range(HEADS)]
    blocks = []
    for n in range(R // SGU_BLOCK):
        vb = vn[n * SGU_BLOCK:(n + 1) * SGU_BLOCK, :]
        mixed = jnp.dot(w_m[0], vb, preferred_element_type=F32)
        for g in range(1, HEADS):
            mg = jnp.dot(w_m[g], vb, preferred_element_type=F32)
            mixed = jnp.where(lane >= g * HEAD_DIM, mg, mixed)
        blocks.append(mixed + bs_ref[...])
    mixed = jnp.concatenate(blocks, axis=0)
    _rms_group_store(od_ref, u * mixed, gd_ref)


def _conv_sgu_call(cv, sg, conv_w, conv_b, ln_g, ln_b, sgu_w, sgu_b, g_c, g_d, S, W):
    T = cv.shape[0]
    R = ROW_TILE
    tiles_per_seq = S // R
    bias_tile = jnp.repeat(sgu_b.T, HEAD_DIM, axis=1)
    row = lambda a: a.reshape(1, W)
    const = lambda shape: pl.BlockSpec(shape, lambda i: tuple(0 for _ in shape))
    return pl.pallas_call(
        functools.partial(_conv_sgu_kernel, tiles_per_seq=tiles_per_seq),
        out_shape=[jax.ShapeDtypeStruct((T, W), BF16)] * 2,
        grid=(T // R,),
        in_specs=[pl.BlockSpec((R, 3 * W), lambda i: (i, 0)),
                  pl.BlockSpec((HALO, 3 * W), lambda i: (jnp.maximum(i * (R // HALO) - 1, 0), 0)),
                  pl.BlockSpec((R, 2 * W), lambda i: (i, 0)),
                  const((CONV_WIDTH, W)), const((1, W)), const((1, W)), const((1, W)),
                  const((HEADS, SGU_BLOCK, SGU_BLOCK)), const((SGU_BLOCK, W)),
                  const((1, W)), const((1, W))],
        out_specs=[pl.BlockSpec((R, W), lambda i: (i, 0))] * 2,
        compiler_params=_cparams(("parallel",), VMEM_LIMIT),
        name="conv_sgu",
    )(cv, cv, sg, conv_w, row(conv_b), row(ln_g), row(ln_b), sgu_w, bias_tile, row(g_c), row(g_d))


ROUTE_CHUNK = ROW_TILE
PAIRS_PER_GROUP = EXPERTS_PER_GROUP * (EXPERTS_PER_GROUP - 1) // 2
N_PAIRS = N_EXPERT_GROUPS * PAIRS_PER_GROUP


def _outproj_router_kernel(ma_ref, mb_ref, mc_ref, md_ref, wo_ref, x_ref, g_ref, b_ref,
                           rwa_ref, rwh_ref, rb_ref, tri_ref, x1_ref, route_ref, cnt_ref, base_ref, *, alpha):
    @pl.when(pl.program_id(0) == 0)
    def _():
        base_ref[...] = jnp.zeros(base_ref.shape, base_ref.dtype)

    base = base_ref[...]
    for c in range(x_ref.shape[0] // ROUTE_CHUNK):
        rows = slice(c * ROUTE_CHUNK, (c + 1) * ROUTE_CHUNK)
        base = _outproj_route_rows(rows, ma_ref, mb_ref, mc_ref, md_ref, wo_ref, x_ref, g_ref, b_ref,
                                   rwa_ref, rwh_ref, rb_ref, tri_ref, x1_ref, route_ref, base, alpha)
    base_ref[...] = base
    cnt_ref[...] = jnp.broadcast_to(base, cnt_ref.shape)


def _outproj_route_rows(rows, ma_ref, mb_ref, mc_ref, md_ref, wo_ref, x_ref, g_ref, b_ref,
                        rwa_ref, rwh_ref, rb_ref, tri_ref, x1_ref, route_ref, base, alpha):
    W = ma_ref.shape[1]
    acc = jnp.dot(ma_ref[rows, :], wo_ref[0:W, :], preferred_element_type=F32)
    for k, ref in enumerate((mb_ref, mc_ref, md_ref), start=1):
        acc = acc + jnp.dot(ref[rows, :], wo_ref[k * W:(k + 1) * W, :], preferred_element_type=F32)
    x1 = _layer_norm_rows(alpha * x_ref[rows, :] + acc, g_ref[...], b_ref[...])
    D = x1.shape[1]
    x1_ref[rows, 0:D] = x1
    xh = x1.astype(BF16)
    xl = (x1 - xh.astype(F32)).astype(BF16)
    both = jnp.dot(xh, rwa_ref[...], preferred_element_type=F32)
    logits = (both[:, 0:ROUTE_LANES] + both[:, ROUTE_LANES:2 * ROUTE_LANES]
              + jnp.dot(xl, rwh_ref[...], preferred_element_type=F32)) + rb_ref[...]
    R = logits.shape[0]
    lane = lax.broadcasted_iota(jnp.int32, (R, ROUTE_LANES), 1).astype(F32)
    big = float(ROUTE_LANES)
    ninf = jnp.finfo(F32).min
    gmask = lane < N_EXPERT_GROUPS
    lg = jnp.where(gmask, logits, ninf)
    mg = jnp.max(lg, axis=-1, keepdims=True)
    sum_g = jnp.sum(jnp.where(gmask, jnp.exp(lg - mg), 0.0), axis=-1, keepdims=True)
    p_gsel = 1.0 / sum_g
    g_sel = jnp.min(jnp.where(lg == mg, lane, big), axis=-1, keepdims=True)
    e_lo = N_EXPERT_GROUPS + EXPERTS_PER_GROUP * g_sel
    emask = (lane >= e_lo) & (lane < e_lo + EXPERTS_PER_GROUP)
    le = jnp.where(emask, logits, ninf)
    me = jnp.max(le, axis=-1, keepdims=True)
    ee = jnp.where(emask, jnp.exp(le - me), 0.0)
    pe = ee / jnp.sum(ee, axis=-1, keepdims=True)
    pe = jnp.where(emask, pe, -1.0)
    p1 = jnp.max(pe, axis=-1, keepdims=True)
    i1 = jnp.min(jnp.where(pe == p1, lane, big), axis=-1, keepdims=True)
    pe2 = jnp.where(lane == i1, -1.0, pe)
    p2 = jnp.max(pe2, axis=-1, keepdims=True)
    i2 = jnp.min(jnp.where(pe2 == p2, lane, big), axis=-1, keepdims=True)
    denom = p1 + p2
    gate1 = p_gsel * p1 / denom
    gate2 = p_gsel * p2 / denom
    a1 = i1 - e_lo
    a2 = i2 - e_lo
    lo = jnp.minimum(a1, a2)
    hi = jnp.maximum(a1, a2)
    first_low = a1 < a2
    gate_lo = jnp.where(first_low, gate1, gate2)
    gate_hi = jnp.where(first_low, gate2, gate1)
    pair = g_sel * PAIRS_PER_GROUP + lo * ((2 * EXPERTS_PER_GROUP - 1) - lo) * 0.5 + (hi - lo - 1.0)
    ex_lo = g_sel * EXPERTS_PER_GROUP + lo
    ex_hi = g_sel * EXPERTS_PER_GROUP + hi
    sel = lane == pair
    one = jnp.where(sel, 1.0, 0.0)
    before = jnp.dot(tri_ref[...], one.astype(BF16), preferred_element_type=F32) + base
    rank = jnp.sum(jnp.where(sel, before, 0.0), axis=-1, keepdims=True)
    route = jnp.where(lane == 0, ex_lo, jnp.where(lane == 1, ex_hi, jnp.where(lane == 2, gate_lo,
            jnp.where(lane == 3, gate_hi, jnp.where(lane == 4, rank, pair)))))
    route_ref[rows, :] = route
    x1_ref[rows, D:D + ROUTE_LANES] = route
    return before[R - 1:R, :] + one[R - 1:R, :]


def _outproj_router_call(mixes, w_out, x, g, b, rw, rb, alpha):
    T, D = x.shape
    W = mixes[0].shape[1]
    R = ROW_TILE
    rw_hi = rw.astype(BF16)
    rw_lo = (rw - rw_hi.astype(F32)).astype(BF16)
    rwa = jnp.concatenate([rw_hi, rw_lo], axis=1)
    idx = jnp.arange(ROUTE_CHUNK)
    tri = (idx[None, :] < idx[:, None]).astype(BF16)
    const = lambda shape: pl.BlockSpec(shape, lambda i: tuple(0 for _ in shape))
    return pl.pallas_call(
        functools.partial(_outproj_router_kernel, alpha=alpha),
        out_shape=[jax.ShapeDtypeStruct((T, D + ROUTE_LANES), F32), jax.ShapeDtypeStruct((T, ROUTE_LANES), F32),
                   jax.ShapeDtypeStruct((8, ROUTE_LANES), F32)],
        grid=(T // R,),
        in_specs=[pl.BlockSpec((R, W), lambda i: (i, 0))] * 4
        + [const((D, D)), pl.BlockSpec((R, D), lambda i: (i, 0)), const((1, D)), const((1, D)),
           const((D, 2 * ROUTE_LANES)), const((D, ROUTE_LANES)), const((1, ROUTE_LANES)),
           const((ROUTE_CHUNK, ROUTE_CHUNK))],
        out_specs=[pl.BlockSpec((R, D + ROUTE_LANES), lambda i: (i, 0)),
                   pl.BlockSpec((R, ROUTE_LANES), lambda i: (i, 0)),
                   const((8, ROUTE_LANES))],
        scratch_shapes=[pltpu.VMEM((1, ROUTE_LANES), F32)],
        compiler_params=_cparams(("arbitrary",), VMEM_LIMIT),
        name="outproj_router",
    )(*mixes, w_out, x, g.reshape(1, D), b.reshape(1, D), rwa, rw_hi, rb, tri)


IDX_SLOTS = 3
SUBLANES = 8


def _dispatch_kernel(fill_ref, dest_hbm, x_ref, xd_hbm, dsm, zbuf, sem_idx, sem_row, sem_fill):
    i = pl.program_id(0)
    n = pl.num_programs(0)
    R = x_ref.shape[0] * SUBLANES
    TB = zbuf.shape[0]
    NI = R

    def idx_copy(t):
        s = t % 2
        return pltpu.make_async_copy(dest_hbm.at[t], dsm.at[pl.ds(s * NI, NI)], sem_idx.at[s])

    def fill_copy(blk):
        return pltpu.make_async_copy(zbuf, xd_hbm.at[pl.ds(blk * TB, TB)], sem_fill)

    @pl.when(i == 0)
    def _():
        idx_copy(0).start()
        zbuf[...] = jnp.zeros(zbuf.shape, zbuf.dtype)

        def start_body(blk, _):
            @pl.when(fill_ref[blk] != 0)
            def _():
                fill_copy(blk).start()
            return 0

        def wait_body(blk, _):
            @pl.when(fill_ref[blk] != 0)
            def _():
                fill_copy(blk).wait()
            return 0

        lax.fori_loop(0, fill_ref.shape[0], start_body, 0)
        lax.fori_loop(0, fill_ref.shape[0], wait_body, 0)

    idx_copy(i).wait()

    @pl.when(i + 1 < n)
    def _():
        idx_copy(i + 1).start()

    base = (i % 2) * NI

    def body(grp, _):
        at = base + SUBLANES * grp
        for u in range(SUBLANES):
            d = dsm[at + u]
            pltpu.make_async_copy(
                x_ref.at[grp, pl.ds(u, 1), :],
                xd_hbm.at[lax.shift_right_logical(d, 3), pl.ds(d & (SUBLANES - 1), 1), :],
                sem_row).start(priority=u % 2)
        return 0

    lax.fori_loop(0, R // SUBLANES, body, 0)
    pltpu.make_async_copy(x_ref, x_ref, sem_row).wait()


def _dispatch_call(fill_flag, dest, x1, n_rows):
    T, D = x1.shape
    R = ROW_TILE
    grid_spec = pltpu.PrefetchScalarGridSpec(
        num_scalar_prefetch=1,
        grid=(T // R,),
        in_specs=[pl.BlockSpec(memory_space=pl.ANY),
                  pl.BlockSpec((R // SUBLANES, SUBLANES, D), lambda i, ff: (i, 0, 0))],
        out_specs=pl.BlockSpec(memory_space=pl.ANY),
        scratch_shapes=[pltpu.SMEM((2 * R,), jnp.int32),
                        pltpu.VMEM((EXPERT_ROWS // SUBLANES, SUBLANES, D), F32),
                        pltpu.SemaphoreType.DMA((2,)),
                        pltpu.SemaphoreType.DMA,
                        pltpu.SemaphoreType.DMA],
    )
    xd = pl.pallas_call(
        _dispatch_kernel,
        out_shape=jax.ShapeDtypeStruct((n_rows // SUBLANES, SUBLANES, D), F32),
        grid_spec=grid_spec,
        compiler_params=_cparams(("arbitrary",), VMEM_LIMIT),
        name="moe_dispatch",
    )(fill_flag, dest, x1.reshape(T // SUBLANES, SUBLANES, D))
    return xd.reshape(n_rows, D)


def _experts_kernel(ea_ref, eb_ref, nact_ref, x_ref, w1a_ref, w3a_ref, w2a_ref, w1b_ref, w3b_ref, w2b_ref,
                    y_ref):
    i = pl.program_id(0)
    D = y_ref.shape[1]

    def expert(xb, w1_ref, w3_ref, w2_ref):
        h1 = jnp.dot(xb, w1_ref[0, 0].astype(BF16), preferred_element_type=F32)
        h3 = jnp.dot(xb, w3_ref[0, 0].astype(BF16), preferred_element_type=F32)
        h = (h1 * jax.nn.sigmoid(h1)) * h3
        return jnp.dot(h.astype(BF16), w2_ref[0, 0].astype(BF16), preferred_element_type=F32)

    @pl.when(i < nact_ref[0])
    def _():
        xb = x_ref[:, 0:D].astype(BF16)
        ya = expert(xb, w1a_ref, w3a_ref, w2a_ref) * x_ref[:, D + 2:D + 3]
        yb = expert(xb, w1b_ref, w3b_ref, w2b_ref) * x_ref[:, D + 3:D + 4]
        y_ref[...] = ya + yb

    @pl.when(i >= nact_ref[0])
    def _():
        y_ref[...] = jnp.zeros(y_ref.shape, y_ref.dtype)


def _experts_call(block_a, block_b, nact, xd, w1, w3, w2, layer):
    P, DX = xd.shape
    D = DX - ROUTE_LANES
    RB = EXPERT_ROWS
    DE = w1.shape[3]
    active = lambda i, ea, eb, na: (jnp.minimum(i, na[0] - 1), 0)
    wa = lambda i, ea, eb, na: (layer, ea[i], 0, 0)
    wb = lambda i, ea, eb, na: (layer, eb[i], 0, 0)
    grid_spec = pltpu.PrefetchScalarGridSpec(
        num_scalar_prefetch=3,
        grid=(P // RB,),
        in_specs=[pl.BlockSpec((RB, DX), active),
                  pl.BlockSpec((1, 1, D, DE), wa), pl.BlockSpec((1, 1, D, DE), wa),
                  pl.BlockSpec((1, 1, DE, D), wa),
                  pl.BlockSpec((1, 1, D, DE), wb), pl.BlockSpec((1, 1, D, DE), wb),
                  pl.BlockSpec((1, 1, DE, D), wb)],
        out_specs=pl.BlockSpec((RB, D), lambda i, ea, eb, na: (i, 0)),
    )
    return pl.pallas_call(
        _experts_kernel,
        out_shape=jax.ShapeDtypeStruct((P, D), F32),
        grid_spec=grid_spec,
        compiler_params=_cparams(("arbitrary",), VMEM_LIMIT),
        name="experts",
    )(block_a, block_b, nact, xd, w1, w3, w2, w1, w3, w2)


def _combine_kernel(dest_hbm, y_hbm, x_ref, g_ref, b_ref, o_ref, dsm, ybuf, sem_idx, sem_row, *, alpha):
    i = pl.program_id(0)
    n = pl.num_programs(0)
    G = x_ref.shape[0]
    D = o_ref.shape[2]
    NI = G * SUBLANES

    def idx_copy(t):
        s = t % IDX_SLOTS
        return pltpu.make_async_copy(dest_hbm.at[t], dsm.at[pl.ds(s * NI, NI)], sem_idx.at[s])

    def issue_gather(t):
        base = (t % IDX_SLOTS) * NI
        s2 = t % 2

        def body(grp, _):
            at = base + SUBLANES * grp
            for u in range(SUBLANES):
                d = dsm[at + u]
                pltpu.make_async_copy(
                    y_hbm.at[lax.shift_right_logical(d, 3), pl.ds(d & (SUBLANES - 1), 1), :],
                    ybuf.at[s2, grp, pl.ds(u, 1), :], sem_row.at[s2]).start(priority=u % 2)
            return 0

        lax.fori_loop(0, G, body, 0)

    @pl.when(i == 0)
    def _():
        c0 = idx_copy(0)
        c0.start()
        c0.wait()
        issue_gather(0)

        @pl.when(n > 1)
        def _():
            idx_copy(1).start()

    @pl.when(i + 1 < n)
    def _():
        idx_copy(i + 1).wait()
        issue_gather(i + 1)

    @pl.when(i + 2 < n)
    def _():
        idx_copy(i + 2).start()

    s2 = i % 2
    pltpu.make_async_copy(ybuf.at[s2], ybuf.at[s2], sem_row.at[s2]).wait()
    o_ref[...] = _layer_norm_rows(alpha * x_ref[:, :, 0:D] + ybuf[s2], g_ref[...], b_ref[...])


def _combine_call(dest, y, x1e, g, b, alpha):
    T = x1e.shape[0]
    D = y.shape[1]
    R = ROW_TILE
    G = R // SUBLANES
    tiles = lambda a: a.reshape(a.shape[0] // SUBLANES, SUBLANES, a.shape[1])
    rows_spec = lambda lanes: pl.BlockSpec((G, SUBLANES, lanes), lambda i: (i, 0, 0))
    vec_spec = pl.BlockSpec((1, 1, D), lambda i: (0, 0, 0))
    out = pl.pallas_call(
        functools.partial(_combine_kernel, alpha=alpha),
        out_shape=jax.ShapeDtypeStruct((T // SUBLANES, SUBLANES, D), F32),
        grid=(T // R,),
        in_specs=[pl.BlockSpec(memory_space=pl.ANY), pl.BlockSpec(memory_space=pl.ANY),
                  rows_spec(x1e.shape[1]), vec_spec, vec_spec],
        out_specs=rows_spec(D),
        scratch_shapes=[pltpu.SMEM((IDX_SLOTS * R,), jnp.int32),
                        pltpu.VMEM((2, G, SUBLANES, D), F32),
                        pltpu.SemaphoreType.DMA((IDX_SLOTS,)),
                        pltpu.SemaphoreType.DMA((2,))],
        compiler_params=_cparams(("arbitrary",), VMEM_LIMIT),
        name="combine_ln",
    )(dest, tiles(y), tiles(x1e), g.reshape(1, 1, D), b.reshape(1, 1, D))
    return out.reshape(T, D)


def _pair_expert_tables():
    lo_hi = [(lo, hi) for lo in range(EXPERTS_PER_GROUP) for hi in range(lo + 1, EXPERTS_PER_GROUP)]
    ea = [g * EXPERTS_PER_GROUP + lo for g in range(N_EXPERT_GROUPS) for lo, _ in lo_hi]
    eb = [g * EXPERTS_PER_GROUP + hi for g in range(N_EXPERT_GROUPS) for _, hi in lo_hi]
    return jnp.asarray(ea, jnp.int32), jnp.asarray(eb, jnp.int32)


def _dispatch_tables(route, cnt, T):
    RB = EXPERT_ROWS
    pairs = jnp.arange(N_PAIRS, dtype=jnp.int32)
    counts = cnt[0, 0:N_PAIRS].astype(jnp.int32)
    padded = (counts + RB - 1) // RB * RB
    p_ends = jnp.cumsum(padded)
    p_starts = p_ends - padded
    rank = route[:, 4].astype(jnp.int32)
    pair = route[:, 5].astype(jnp.int32)
    start_of = jnp.sum(jnp.where(pair[:, None] == pairs[None, :], p_starts[None, :], 0), axis=-1)
    dest = (start_of + rank).reshape(T // ROW_TILE, ROW_TILE)
    P = -(-T // RB) * RB + N_PAIRS * RB
    n_blocks = P // RB
    blocks = jnp.arange(n_blocks, dtype=jnp.int32)
    block_pair = jnp.minimum(
        jnp.sum((p_ends[None, :] <= (blocks * RB)[:, None]).astype(jnp.int32), axis=1), N_PAIRS - 1)
    ea, eb = _pair_expert_tables()
    pick = block_pair[:, None] == pairs[None, :]
    block_a = jnp.sum(jnp.where(pick, ea[None, :], 0), axis=1)
    block_b = jnp.sum(jnp.where(pick, eb[None, :], 0), axis=1)
    nact = (p_ends[-1] // RB).astype(jnp.int32).reshape(1)
    last_partial = jnp.any((blocks[:, None] == (p_ends // RB - 1)[None, :]) & ((counts % RB) != 0)[None, :], axis=1)
    fill_flag = (last_partial | (blocks >= nact[0])).astype(jnp.int32)
    return dest, block_a, block_b, nact, fill_flag, P


def kernel(x, ln_in_g, ln_in_b, w_in, b_f, conv_w, conv_b, sgu_ln_g, sgu_ln_b, sgu_w, sgu_b, grp_g, w_out, ln1_g, ln1_b, router_g_w, router_g_b, router_e_w, router_e_b, w1, w3, w2, ln2_g, ln2_b):
    B, S, D = x.shape
    depth = w_in.shape[0]
    W = D // N_MIXERS
    T = B * S
    alpha = (2.0 * depth) ** 0.25
    scale = 1.0 / math.sqrt(HEAD_DIM)

    xs = x.reshape(T, D)
    for l in range(depth):
        wl = w_in[l]
        o_f = 6 * W
        o_cv = o_f + HEADS
        o_sg = o_cv + 3 * W
        f_cols = jnp.pad(wl[:, o_f:o_cv], ((0, 0), (0, LANES - HEADS)))
        w_all = jnp.concatenate(
            [wl[:, 0:W] * scale, wl[:, W:3 * W], wl[:, 3 * W:4 * W] * scale, wl[:, 4 * W:6 * W],
             wl[:, o_cv:o_sg], wl[:, o_sg:], f_cols], axis=1).astype(BF16)
        rw = jnp.pad(jnp.concatenate([router_g_w[l], router_e_w[l]], axis=1),
                     ((0, 0), (0, ROUTE_LANES - N_EXPERT_GROUPS - N_EXPERTS)))
        rb = jnp.pad(jnp.concatenate([router_g_b[l], router_e_b[l]]),
                     (0, ROUTE_LANES - N_EXPERT_GROUPS - N_EXPERTS)).reshape(1, ROUTE_LANES)
        gains = [grp_g[l, k * W:(k + 1) * W].reshape(1, W) for k in range(N_MIXERS)]

        if l == 0:
            qkv_a, qkv_b, cv, sg, f_pad, xs = _inproj_call(xs, w_all, W, entry_ln=(ln_in_g, ln_in_b))
        else:
            qkv_a, qkv_b, cv, sg, f_pad = _inproj_call(xs, w_all, W)
        f_rows = f_pad[:, 0:HEADS].reshape(B, S, HEADS).transpose(0, 2, 1).reshape(B * HEADS, S)
        b_rows = jnp.tile(b_f[l], B).reshape(B * HEADS, 1)
        c = _fcum_call(f_rows, b_rows)
        c_tiles = c.reshape(B, HEADS * (S // ATT_K), ATT_K)
        mix_a = _sb_attn_call(qkv_a, gains[0], B, S, W)
        mix_b = _fox_attn_call(qkv_b, c_tiles, gains[1], B, S, W)
        mix_c, mix_d = _conv_sgu_call(cv, sg, conv_w[l], conv_b[l], sgu_ln_g[l], sgu_ln_b[l],
                                      sgu_w[l], sgu_b[l], gains[2][0], gains[3][0], S, W)
        x1e, route, cnt = _outproj_router_call([mix_a, mix_b, mix_c, mix_d], w_out[l].astype(BF16), xs,
                                               ln1_g[l], ln1_b[l], rw, rb, alpha)
        dest, block_a, block_b, nact, fill_flag, n_rows = _dispatch_tables(route, cnt, T)
        xd = _dispatch_call(fill_flag, dest, x1e, n_rows)
        y = _experts_call(block_a, block_b, nact, xd, w1, w3, w2, l)
        xs = _combine_call(dest, y, x1e, ln2_g[l], ln2_b[l], alpha)
    return xs.reshape(B, S, D)
```

```python
import functools
import math

import jax
import jax.numpy as jnp
from jax import lax
from jax.experimental import pallas as pl
from jax.experimental.pallas import tpu as pltpu

N_MIXERS = 4
HEAD_DIM = 64
HEADS = 4
CONV_WIDTH = 3
SGU_BLOCK = 128
CHUNK = 64
N_EXPERT_GROUPS = 4
EXPERTS_PER_GROUP = 8
N_EXPERTS = N_EXPERT_GROUPS * EXPERTS_PER_GROUP
TOP_K = 2
LN_EPS = 1e-5
RMS_EPS = 1e-6

LANES = 128
V7X_VMEM_BYTES = 64 * 1024 * 1024
VMEM_LIMIT = 48 * 1024 * 1024
EXPERT_VMEM_LIMIT = 56 * 1024 * 1024

ROW_TILE = 512
DISPATCH_TILE = 1024
ATT_Q = 256
ATT_K = 256
EXPERT_ROWS = 128
ROUTE_LANES = LANES

F32 = jnp.float32
BF16 = jnp.bfloat16


def _cparams(sem, vmem=None):
    return pltpu.CompilerParams(dimension_semantics=sem, vmem_limit_bytes=vmem)


def _layer_norm_rows(y, g, b):
    mu = jnp.mean(y, axis=-1, keepdims=True)
    d = y - mu
    var = jnp.mean(d * d, axis=-1, keepdims=True)
    return d * lax.rsqrt(var + LN_EPS) * g + b


def _softplus(z):
    return jnp.maximum(z, 0.0) + jnp.log1p(jnp.exp(-jnp.abs(z)))


def _inproj_kernel(x_ref, w_ref, *refs, widths, entry_norm):
    if entry_norm:
        g_ref, b_ref, qa_ref, qb_ref, cv_ref, sg_ref, f_ref, xn_ref = refs
        x = _layer_norm_rows(x_ref[...], g_ref[...], b_ref[...])
        xn_ref[...] = x
    else:
        qa_ref, qb_ref, cv_ref, sg_ref, f_ref = refs
        x = x_ref[...]
    xb = x.astype(BF16)
    off = 0
    for ref, wd in zip((qa_ref, qb_ref, cv_ref, sg_ref, f_ref), widths):
        ref[...] = jnp.dot(xb, w_ref[:, off:off + wd], preferred_element_type=F32).astype(ref.dtype)
        off += wd


def _inproj_call(x, w_all, W, entry_ln=None):
    T, D = x.shape
    widths = (3 * W, 3 * W, 3 * W, 2 * W, LANES)
    NC = sum(widths)
    dts = (BF16, BF16, F32, F32, F32)
    row_spec = pl.BlockSpec((ROW_TILE, D), lambda i: (i, 0))
    vec_spec = pl.BlockSpec((1, D), lambda i: (0, 0))
    in_specs = [row_spec, pl.BlockSpec((D, NC), lambda i: (0, 0))]
    out_shape = [jax.ShapeDtypeStruct((T, wd), dt) for wd, dt in zip(widths, dts)]
    out_specs = [pl.BlockSpec((ROW_TILE, wd), lambda i: (i, 0)) for wd in widths]
    args = [x, w_all]
    if entry_ln is not None:
        in_specs += [vec_spec, vec_spec]
        out_shape.append(jax.ShapeDtypeStruct((T, D), F32))
        out_specs.append(row_spec)
        args += [entry_ln[0].reshape(1, D), entry_ln[1].reshape(1, D)]
    return pl.pallas_call(
        functools.partial(_inproj_kernel, widths=widths, entry_norm=entry_ln is not None),
        out_shape=out_shape,
        grid=(T // ROW_TILE,),
        in_specs=in_specs,
        out_specs=out_specs,
        compiler_params=_cparams(("parallel",), VMEM_LIMIT),
        name="inproj",
    )(*args)


def _fcum_kernel(f_ref, b_ref, tri_ref, c_ref):
    R, S = f_ref.shape
    y = f_ref[...] + b_ref[...]
    lf = -_softplus(-y)
    tri = tri_ref[...]
    carry = jnp.zeros((R, 1), F32)
    for blk in range(S // LANES):
        seg = lf[:, blk * LANES:(blk + 1) * LANES]
        s1 = seg.astype(BF16)
        r1 = seg - s1.astype(F32)
        s2 = r1.astype(BF16)
        s3 = (r1 - s2.astype(F32)).astype(BF16)
        cs = (jnp.dot(s1, tri, preferred_element_type=F32)
              + jnp.dot(s2, tri, preferred_element_type=F32)
              + jnp.dot(s3, tri, preferred_element_type=F32)) + carry
        c_ref[:, blk * LANES:(blk + 1) * LANES] = cs
        carry = cs[:, LANES - 1:LANES]


def _fcum_call(f_rows, b_rows):
    R, S = f_rows.shape
    idx = jnp.arange(LANES)
    tri = (idx[:, None] <= idx[None, :]).astype(BF16)
    return pl.pallas_call(
        _fcum_kernel,
        out_shape=jax.ShapeDtypeStruct((R, S), F32),
        name="forget_cumsum",
    )(f_rows, b_rows, tri)


def _rms_group_store(o_ref, out, g_ref):
    ms = jnp.mean(out * out, axis=-1, keepdims=True)
    o_ref[...] = (out * lax.rsqrt(ms + RMS_EPS) * g_ref[...]).astype(o_ref.dtype)


def _head_masks():
    lane = lax.broadcasted_iota(jnp.int32, (1, LANES), 1)
    return [lane < HEAD_DIM, lane >= HEAD_DIM]


def _nt_dot(a, b):
    return lax.dot_general(a, b, (((1,), (1,)), ((), ())), preferred_element_type=F32)


SB_DEAD = -104.0


def _neg_softplus(z):
    return jnp.minimum(-z, 0.0) - jnp.log(1.0 + jnp.exp(-jnp.abs(z)))


def _sb_attn_kernel(q_ref, k_ref, v_ref, m_ref, g_ref, o_ref):
    i = pl.program_id(1)
    Q = q_ref.shape[0]
    row = lax.broadcasted_iota(jnp.int32, (Q, ATT_K), 0)
    col = lax.broadcasted_iota(jnp.int32, (Q, ATT_K), 1)
    diag_mask = col < row
    hmasks = _head_masks()

    def tile(qms, j, state, mask):
        ks = pl.multiple_of(j * ATT_K, ATT_K)
        k_ts = [k_ref[pl.ds(ks, ATT_K), p * LANES:(p + 1) * LANES] for p in range(HEADS // 2)]
        v_ts = [v_ref[pl.ds(ks, ATT_K), p * LANES:(p + 1) * LANES] for p in range(HEADS // 2)]
        later_mat = m_ref[...]
        zs = [_nt_dot(qms[h], k_ts[h // 2]) for h in range(HEADS)]
        mid = []
        for h in range(HEADS):
            log_1m = _neg_softplus(zs[h])
            log_b = zs[h] + log_1m
            if mask is not None:
                log_1m = jnp.where(mask, log_1m, 0.0)
            hi = log_1m.astype(BF16)
            lo = (log_1m - hi.astype(F32)).astype(BF16)
            later = (jnp.dot(hi, later_mat, preferred_element_type=F32)
                     + jnp.dot(lo, later_mat, preferred_element_type=F32)) + state[2 * h]
            mid.append((log_b, later, log_1m[:, 0:1]))
        out = []
        for h in range(HEADS):
            log_b, later, first = mid[h]
            w = jnp.exp(log_b + later)
            if mask is not None:
                w = jnp.where(mask, w, 0.0)
            acc = state[2 * h + 1] + jnp.dot(w.astype(BF16), v_ts[h // 2], preferred_element_type=F32)
            out.extend((later[:, 0:1] + first, acc))
        return tuple(out)

    qms = []
    for h in range(HEADS):
        q_pair = q_ref[:, (h // 2) * LANES:(h // 2 + 1) * LANES]
        qms.append(jnp.where(hmasks[h % 2], q_pair, jnp.zeros_like(q_pair)))
    def alive(state):
        top = state[0]
        for h in range(1, HEADS):
            top = jnp.maximum(top, state[2 * h])
        return (jnp.max(top) >= SB_DEAD).astype(jnp.int32)

    state = (jnp.zeros((Q, 1), F32), jnp.zeros((Q, LANES), F32)) * HEADS
    state = tile(qms, i, state, diag_mask)

    def cond(c):
        return (c[0] <= i) & (c[1] > 0)

    def body(c):
        st = tile(qms, i - c[0], tuple(c[2:]), None)
        return (c[0] + 1, alive(st)) + st

    state = lax.while_loop(cond, body, (jnp.int32(1), alive(state)) + state)[2:]
    outs = [jnp.where(hmasks[0], state[4 * p + 1], state[4 * p + 3]) for p in range(HEADS // 2)]
    _rms_group_store(o_ref, jnp.concatenate(outs, axis=-1), g_ref)


def _fox_attn_kernel(q_ref, k_ref, v_ref, c_ref, g_ref, o_ref):
    i = pl.program_id(1)
    Q = q_ref.shape[0]
    nk = c_ref.shape[1] // HEADS
    row = lax.broadcasted_iota(jnp.int32, (Q, ATT_K), 0)
    col = lax.broadcasted_iota(jnp.int32, (Q, ATT_K), 1)
    diag_mask = col <= row
    hmasks = _head_masks()
    neg = jnp.finfo(F32).min

    def tile(qms, j, state, mask):
        ks = pl.multiple_of(j * ATT_K, ATT_K)
        k_ts = [k_ref[pl.ds(ks, ATT_K), p * LANES:(p + 1) * LANES] for p in range(HEADS // 2)]
        v_ts = [v_ref[pl.ds(ks, ATT_K), p * LANES:(p + 1) * LANES] for p in range(HEADS // 2)]
        ss = []
        for h in range(HEADS):
            c_row = c_ref[0, pl.ds(h * nk + j, 1), :]
            s = _nt_dot(qms[h], k_ts[h // 2]) - c_row
            if mask is not None:
                s = jnp.where(mask, s, neg)
            ss.append(s)
        mid = []
        for h in range(HEADS):
            m, l = state[3 * h], state[3 * h + 1]
            m_new = jnp.maximum(m, jnp.max(ss[h], axis=-1, keepdims=True))
            alpha = jnp.exp(m - m_new)
            p = jnp.exp(ss[h] - m_new)
            l = alpha * l + jnp.sum(p, axis=-1, keepdims=True)
            mid.append((m_new, l, alpha, p.astype(BF16)))
        out = []
        for h in range(HEADS):
            m_new, l, alpha, p = mid[h]
            acc = alpha * state[3 * h + 2] + jnp.dot(p, v_ts[h // 2], preferred_element_type=F32)
            out.extend((m_new, l, acc))
        return tuple(out)

    qms = []
    for h in range(HEADS):
        q_pair = q_ref[:, (h // 2) * LANES:(h // 2 + 1) * LANES]
        qms.append(jnp.where(hmasks[h % 2], q_pair, jnp.zeros_like(q_pair)))
    state = (jnp.full((Q, 1), neg, F32), jnp.zeros((Q, 1), F32), jnp.zeros((Q, LANES), F32)) * HEADS
    state = tile(qms, i, state, diag_mask)
    state = lax.fori_loop(1, i + 1, lambda jj, st: tile(qms, i - jj, st, None), state)
    heads = [state[3 * h + 2] / state[3 * h + 1] for h in range(HEADS)]
    outs = [jnp.where(hmasks[0], heads[2 * p], heads[2 * p + 1]) for p in range(HEADS // 2)]
    _rms_group_store(o_ref, jnp.concatenate(outs, axis=-1), g_ref)


def _attn_specs(B, S, W):
    nq = S // ATT_Q
    q_spec = pl.BlockSpec((ATT_Q, W), lambda b, i: (b * nq + i, 0))
    k_spec = pl.BlockSpec((S, W), lambda b, i: (b, 1))
    v_spec = pl.BlockSpec((S, W), lambda b, i: (b, 2))
    g_spec = pl.BlockSpec((1, W), lambda b, i: (0, 0))
    o_spec = pl.BlockSpec((ATT_Q, W), lambda b, i: (b * nq + i, 0))
    return nq, q_spec, k_spec, v_spec, g_spec, o_spec


def _sb_attn_call(qkv, gain, B, S, W):
    T = B * S
    nq, q_spec, k_spec, v_spec, g_spec, o_spec = _attn_specs(B, S, W)
    idx = jnp.arange(ATT_K)
    later_mat = (idx[:, None] > idx[None, :]).astype(BF16)
    return pl.pallas_call(
        _sb_attn_kernel,
        out_shape=jax.ShapeDtypeStruct((T, W), BF16),
        grid=(B, nq),
        in_specs=[q_spec, k_spec, v_spec,
                  pl.BlockSpec((ATT_K, ATT_K), lambda b, i: (0, 0)), g_spec],
        out_specs=o_spec,
        compiler_params=_cparams(("parallel", "arbitrary"), VMEM_LIMIT),
        name="sb_attn",
    )(qkv, qkv, qkv, later_mat, gain)


def _fox_attn_call(qkv, c_tiles, gain, B, S, W):
    T = B * S
    nq, q_spec, k_spec, v_spec, g_spec, o_spec = _attn_specs(B, S, W)
    rows = c_tiles.shape[1]
    return pl.pallas_call(
        _fox_attn_kernel,
        out_shape=jax.ShapeDtypeStruct((T, W), BF16),
        grid=(B, nq),
        in_specs=[q_spec, k_spec, v_spec,
                  pl.BlockSpec((1, rows, ATT_K), lambda b, i: (b, 0, 0)), g_spec],
        out_specs=o_spec,
        compiler_params=_cparams(("parallel", "arbitrary"), VMEM_LIMIT),
        name="fox_attn",
    )(qkv, qkv, qkv, c_tiles, gain)


HALO = 8


def _gelu_tanh(x):
    c = math.sqrt(2.0 / math.pi)
    return 0.5 * x * (1.0 + jnp.tanh(c * (x + 0.044715 * (x * x * x))))


def _conv_sgu_kernel(cv_ref, halo_ref, sg_ref, cw_ref, cb_ref, lg_ref, lb_ref, ws_ref, bs_ref,
                     gc_ref, gd_ref, oc_ref, od_ref, *, tiles_per_seq):
    R, W3 = cv_ref.shape
    W = W3 // 3
    i = pl.program_id(0)
    z = cv_ref[:, 2 * W:3 * W] * cv_ref[:, 0:W]
    zh = halo_ref[:, 2 * W:3 * W] * halo_ref[:, 0:W]
    zh = jnp.where(i % tiles_per_seq == 0, jnp.zeros_like(zh), zh)
    zf = jnp.concatenate([zh, z], axis=0)
    z1 = pltpu.roll(zf, 1, 0)[HALO:]
    z2 = pltpu.roll(zf, 2, 0)[HALO:]
    y = cb_ref[...] + cw_ref[0:1, :] * z2
    y = y + cw_ref[1:2, :] * z1
    y = y + cw_ref[2:3, :] * z
    _rms_group_store(oc_ref, cv_ref[:, W:2 * W] * y, gc_ref)
    gel = _gelu_tanh(sg_ref[...])
    u = gel[:, 0:W]
    vn = _layer_norm_rows(gel[:, W:2 * W], lg_ref[...], lb_ref[...]).astype(BF16)
    lane = lax.broadcasted_iota(jnp.int32, (1, W), 1)
    pi = lax.broadcasted_iota(jnp.int32, (SGU_BLOCK, SGU_BLOCK), 0) // CHUNK
    pj = lax.broadcasted_iota(jnp.int32, (SGU_BLOCK, SGU_BLOCK), 1) // CHUNK
    w_m = [jnp.where(pj <= pi, ws_ref[g], 0.0).astype(BF16) for g in range(HEADS)]
    blocks = []
    for n in range(R // SGU_BLOCK):
        vb = vn[n * SGU_BLOCK:(n + 1) * SGU_BLOCK, :]
        mixed = jnp.dot(w_m[0], vb, preferred_element_type=F32)
        for g in range(1, HEADS):
            mg = jnp.dot(w_m[g], vb, preferred_element_type=F32)
            mixed = jnp.where(lane >= g * HEAD_DIM, mg, mixed)
        blocks.append(mixed + bs_ref[...])
    mixed = jnp.concatenate(blocks, axis=0)
    _rms_group_store(od_ref, u * mixed, gd_ref)


def _conv_sgu_call(cv, sg, conv_w, conv_b, ln_g, ln_b, sgu_w, sgu_b, g_c, g_d, S, W):
    T = cv.shape[0]
    R = ROW_TILE
    tiles_per_seq = S // R
    bias_tile = jnp.repeat(sgu_b.T, HEAD_DIM, axis=1)
    row = lambda a: a.reshape(1, W)
    const = lambda shape: pl.BlockSpec(shape, lambda i: tuple(0 for _ in shape))
    return pl.pallas_call(
        functools.partial(_conv_sgu_kernel, tiles_per_seq=tiles_per_seq),
        out_shape=[jax.ShapeDtypeStruct((T, W), BF16)] * 2,
        grid=(T // R,),
        in_specs=[pl.BlockSpec((R, 3 * W), lambda i: (i, 0)),
                  pl.BlockSpec((HALO, 3 * W), lambda i: (jnp.maximum(i * (R // HALO) - 1, 0), 0)),
                  pl.BlockSpec((R, 2 * W), lambda i: (i, 0)),
                  const((CONV_WIDTH, W)), const((1, W)), const((1, W)), const((1, W)),
                  const((HEADS, SGU_BLOCK, SGU_BLOCK)), const((SGU_BLOCK, W)),
                  const((1, W)), const((1, W))],
        out_specs=[pl.BlockSpec((R, W), lambda i: (i, 0))] * 2,
        compiler_params=_cparams(("parallel",), VMEM_LIMIT),
        name="conv_sgu",
    )(cv, cv, sg, conv_w, row(conv_b), row(ln_g), row(ln_b), sgu_w, bias_tile, row(g_c), row(g_d))


ROUTE_CHUNK = ROW_TILE
PAIRS_PER_GROUP = EXPERTS_PER_GROUP * (EXPERTS_PER_GROUP - 1) // 2
N_PAIRS = N_EXPERT_GROUPS * PAIRS_PER_GROUP


def _outproj_router_kernel(ma_ref, mb_ref, mc_ref, md_ref, wo_ref, x_ref, g_ref, b_ref,
                           rwa_ref, rwh_ref, rb_ref, tri_ref, x1_ref, route_ref, cnt_ref, base_ref, *, alpha):
    @pl.when(pl.program_id(0) == 0)
    def _():
        base_ref[...] = jnp.zeros(base_ref.shape, base_ref.dtype)

    base = base_ref[...]
    for c in range(x_ref.shape[0] // ROUTE_CHUNK):
        rows = slice(c * ROUTE_CHUNK, (c + 1) * ROUTE_CHUNK)
        base = _outproj_route_rows(rows, ma_ref, mb_ref, mc_ref, md_ref, wo_ref, x_ref, g_ref, b_ref,
                                   rwa_ref, rwh_ref, rb_ref, tri_ref, x1_ref, route_ref, base, alpha)
    base_ref[...] = base
    cnt_ref[...] = jnp.broadcast_to(base, cnt_ref.shape)


def _outproj_route_rows(rows, ma_ref, mb_ref, mc_ref, md_ref, wo_ref, x_ref, g_ref, b_ref,
                        rwa_ref, rwh_ref, rb_ref, tri_ref, x1_ref, route_ref, base, alpha):
    W = ma_ref.shape[1]
    acc = jnp.dot(ma_ref[rows, :], wo_ref[0:W, :], preferred_element_type=F32)
    for k, ref in enumerate((mb_ref, mc_ref, md_ref), start=1):
        acc = acc + jnp.dot(ref[rows, :], wo_ref[k * W:(k + 1) * W, :], preferred_element_type=F32)
    x1 = _layer_norm_rows(alpha * x_ref[rows, :] + acc, g_ref[...], b_ref[...])
    D = x1.shape[1]
    x1_ref[rows, 0:D] = x1
    xh = x1.astype(BF16)
    xl = (x1 - xh.astype(F32)).astype(BF16)
    both = jnp.dot(xh, rwa_ref[...], preferred_element_type=F32)
    logits = (both[:, 0:ROUTE_LANES] + both[:, ROUTE_LANES:2 * ROUTE_LANES]
              + jnp.dot(xl, rwh_ref[...], preferred_element_type=F32)) + rb_ref[...]
    R = logits.shape[0]
    lane = lax.broadcasted_iota(jnp.int32, (R, ROUTE_LANES), 1).astype(F32)
    big = float(ROUTE_LANES)
    ninf = jnp.finfo(F32).min
    gmask = lane < N_EXPERT_GROUPS
    lg = jnp.where(gmask, logits, ninf)
    mg = jnp.max(lg, axis=-1, keepdims=True)
    sum_g = jnp.sum(jnp.where(gmask, jnp.exp(lg - mg), 0.0), axis=-1, keepdims=True)
    p_gsel = 1.0 / sum_g
    g_sel = jnp.min(jnp.where(lg == mg, lane, big), axis=-1, keepdims=True)
    e_lo = N_EXPERT_GROUPS + EXPERTS_PER_GROUP * g_sel
    emask = (lane >= e_lo) & (lane < e_lo + EXPERTS_PER_GROUP)
    le = jnp.where(emask, logits, ninf)
    me = jnp.max(le, axis=-1, keepdims=True)
    ee = jnp.where(emask, jnp.exp(le - me), 0.0)
    pe = ee / jnp.sum(ee, axis=-1, keepdims=True)
    pe = jnp.where(emask, pe, -1.0)
    p1 = jnp.max(pe, axis=-1, keepdims=True)
    i1 = jnp.min(jnp.where(pe == p1, lane, big), axis=-1, keepdims=True)
    pe2 = jnp.where(lane == i1, -1.0, pe)
    p2 = jnp.max(pe2, axis=-1, keepdims=True)
    i2 = jnp.min(jnp.where(pe2 == p2, lane, big), axis=-1, keepdims=True)
    denom = p1 + p2
    gate1 = p_gsel * p1 / denom
    gate2 = p_gsel * p2 / denom
    a1 = i1 - e_lo
    a2 = i2 - e_lo
    lo = jnp.minimum(a1, a2)
    hi = jnp.maximum(a1, a2)
    first_low = a1 < a2
    gate_lo = jnp.where(first_low, gate1, gate2)
    gate_hi = jnp.where(first_low, gate2, gate1)
    pair = g_sel * PAIRS_PER_GROUP + lo * ((2 * EXPERTS_PER_GROUP - 1) - lo) * 0.5 + (hi - lo - 1.0)
    ex_lo = g_sel * EXPERTS_PER_GROUP + lo
    ex_hi = g_sel * EXPERTS_PER_GROUP + hi
    sel = lane == pair
    one = jnp.where(sel, 1.0, 0.0)
    before = jnp.dot(tri_ref[...], one.astype(BF16), preferred_element_type=F32) + base
    rank = jnp.sum(jnp.where(sel, before, 0.0), axis=-1, keepdims=True)
    route = jnp.where(lane == 0, ex_lo, jnp.where(lane == 1, ex_hi, jnp.where(lane == 2, gate_lo,
            jnp.where(lane == 3, gate_hi, jnp.where(lane == 4, rank, pair)))))
    route_ref[rows, :] = route
    x1_ref[rows, D:D + ROUTE_LANES] = route
    return before[R - 1:R, :] + one[R - 1:R, :]


def _outproj_router_call(mixes, w_out, x, g, b, rw, rb, alpha):
    T, D = x.shape
    W = mixes[0].shape[1]
    R = ROW_TILE
    rw_hi = rw.astype(BF16)
    rw_lo = (rw - rw_hi.astype(F32)).astype(BF16)
    rwa = jnp.concatenate([rw_hi, rw_lo], axis=1)
    idx = jnp.arange(ROUTE_CHUNK)
    tri = (idx[None, :] < idx[:, None]).astype(BF16)
    const = lambda shape: pl.BlockSpec(shape, lambda i: tuple(0 for _ in shape))
    return pl.pallas_call(
        functools.partial(_outproj_router_kernel, alpha=alpha),
        out_shape=[jax.ShapeDtypeStruct((T, D + ROUTE_LANES), F32), jax.ShapeDtypeStruct((T, ROUTE_LANES), F32),
                   jax.ShapeDtypeStruct((8, ROUTE_LANES), F32)],
        grid=(T // R,),
        in_specs=[pl.BlockSpec((R, W), lambda i: (i, 0))] * 4
        + [const((D, D)), pl.BlockSpec((R, D), lambda i: (i, 0)), const((1, D)), const((1, D)),
           const((D, 2 * ROUTE_LANES)), const((D, ROUTE_LANES)), const((1, ROUTE_LANES)),
           const((ROUTE_CHUNK, ROUTE_CHUNK))],
        out_specs=[pl.BlockSpec((R, D + ROUTE_LANES), lambda i: (i, 0)),
                   pl.BlockSpec((R, ROUTE_LANES), lambda i: (i, 0)),
                   const((8, ROUTE_LANES))],
        scratch_shapes=[pltpu.VMEM((1, ROUTE_LANES), F32)],
        compiler_params=_cparams(("arbitrary",), VMEM_LIMIT),
        name="outproj_router",
    )(*mixes, w_out, x, g.reshape(1, D), b.reshape(1, D), rwa, rw_hi, rb, tri)


IDX_SLOTS = 3
SUBLANES = 8


def _dispatch_kernel(fill_ref, dest_hbm, x_ref, xd_hbm, dsm, zbuf, sem_idx, sem_row, sem_fill):
    i = pl.program_id(0)
    n = pl.num_programs(0)
    R = x_ref.shape[0] * SUBLANES
    TB = zbuf.shape[0]
    NI = R

    def idx_copy(t):
        s = t % 2
        return pltpu.make_async_copy(dest_hbm.at[t], dsm.at[pl.ds(s * NI, NI)], sem_idx.at[s])

    def fill_copy(blk):
        return pltpu.make_async_copy(zbuf, xd_hbm.at[pl.ds(blk * TB, TB)], sem_fill)

    @pl.when(i == 0)
    def _():
        idx_copy(0).start()
        zbuf[...] = jnp.zeros(zbuf.shape, zbuf.dtype)

        def start_body(blk, _):
            @pl.when(fill_ref[blk] != 0)
            def _():
                fill_copy(blk).start()
            return 0

        def wait_body(blk, _):
            @pl.when(fill_ref[blk] != 0)
            def _():
                fill_copy(blk).wait()
            return 0

        lax.fori_loop(0, fill_ref.shape[0], start_body, 0)
        lax.fori_loop(0, fill_ref.shape[0], wait_body, 0)

    idx_copy(i).wait()

    @pl.when(i + 1 < n)
    def _():
        idx_copy(i + 1).start()

    base = (i % 2) * NI

    def body(grp, _):
        at = base + SUBLANES * grp
        for u in range(SUBLANES):
            d = dsm[at + u]
            pltpu.make_async_copy(
                x_ref.at[grp, pl.ds(u, 1), :],
                xd_hbm.at[lax.shift_right_logical(d, 3), pl.ds(d & (SUBLANES - 1), 1), :],
                sem_row).start(priority=u % 2)
        return 0

    lax.fori_loop(0, R // SUBLANES, body, 0)
    pltpu.make_async_copy(x_ref, x_ref, sem_row).wait()


def _dispatch_call(fill_flag, dest, x1, n_rows):
    T, D = x1.shape
    R = DISPATCH_TILE
    dest = dest.reshape(T // R, R)
    grid_spec = pltpu.PrefetchScalarGridSpec(
        num_scalar_prefetch=1,
        grid=(T // R,),
        in_specs=[pl.BlockSpec(memory_space=pl.ANY),
                  pl.BlockSpec((R // SUBLANES, SUBLANES, D), lambda i, ff: (i, 0, 0))],
        out_specs=pl.BlockSpec(memory_space=pl.ANY),
        scratch_shapes=[pltpu.SMEM((2 * R,), jnp.int32),
                        pltpu.VMEM((EXPERT_ROWS // SUBLANES, SUBLANES, D), F32),
                        pltpu.SemaphoreType.DMA((2,)),
                        pltpu.SemaphoreType.DMA,
                        pltpu.SemaphoreType.DMA],
    )
    xd = pl.pallas_call(
        _dispatch_kernel,
        out_shape=jax.ShapeDtypeStruct((n_rows // SUBLANES, SUBLANES, D), F32),
        grid_spec=grid_spec,
        compiler_params=_cparams(("arbitrary",), VMEM_LIMIT),
        name="moe_dispatch",
    )(fill_flag, dest, x1.reshape(T // SUBLANES, SUBLANES, D))
    return xd.reshape(n_rows, D)


def _experts_kernel(ea_ref, eb_ref, nact_ref, x_ref, w1_ref, w3_ref, w2_ref, y_ref, wb1, wb3, wb2):
    i = pl.program_id(0)
    D = y_ref.shape[1]
    group = ea_ref[i] // EXPERTS_PER_GROUP
    prev = ea_ref[jnp.maximum(i, 1) - 1] // EXPERTS_PER_GROUP

    @pl.when((i == 0) | (group != prev))
    def _():
        for e in range(EXPERTS_PER_GROUP):
            wb1[e] = w1_ref[0, e].astype(BF16)
            wb3[e] = w3_ref[0, e].astype(BF16)
            wb2[e] = w2_ref[0, e].astype(BF16)

    def expert(xb, e):
        h1 = jnp.dot(xb, wb1[e], preferred_element_type=F32)
        h3 = jnp.dot(xb, wb3[e], preferred_element_type=F32)
        h = (h1 * jax.nn.sigmoid(h1)) * h3
        return jnp.dot(h.astype(BF16), wb2[e], preferred_element_type=F32)

    @pl.when(i < nact_ref[0])
    def _():
        xb = x_ref[:, 0:D].astype(BF16)
        ya = expert(xb, ea_ref[i] % EXPERTS_PER_GROUP) * x_ref[:, D + 2:D + 3]
        yb = expert(xb, eb_ref[i] % EXPERTS_PER_GROUP) * x_ref[:, D + 3:D + 4]
        y_ref[...] = ya + yb

    @pl.when(i >= nact_ref[0])
    def _():
        y_ref[...] = jnp.zeros(y_ref.shape, y_ref.dtype)


def _experts_call(block_a, block_b, nact, xd, w1, w3, w2, layer):
    P, DX = xd.shape
    D = DX - ROUTE_LANES
    RB = EXPERT_ROWS
    DE = w1.shape[3]
    EG = EXPERTS_PER_GROUP
    by_group = lambda w: w.reshape(w.shape[0] * N_EXPERT_GROUPS, EG, w.shape[2], w.shape[3])
    active = lambda i, ea, eb, na: (jnp.minimum(i, na[0] - 1), 0)
    wmap = lambda i, ea, eb, na: (layer * N_EXPERT_GROUPS + ea[i] // EG, 0, 0, 0)
    once = pl.Buffered(1)
    grid_spec = pltpu.PrefetchScalarGridSpec(
        num_scalar_prefetch=3,
        grid=(P // RB,),
        in_specs=[pl.BlockSpec((RB, DX), active),
                  pl.BlockSpec((1, EG, D, DE), wmap, pipeline_mode=once),
                  pl.BlockSpec((1, EG, D, DE), wmap, pipeline_mode=once),
                  pl.BlockSpec((1, EG, DE, D), wmap, pipeline_mode=once)],
        out_specs=pl.BlockSpec((RB, D), lambda i, ea, eb, na: (i, 0)),
        scratch_shapes=[pltpu.VMEM((EG, D, DE), BF16), pltpu.VMEM((EG, D, DE), BF16),
                        pltpu.VMEM((EG, DE, D), BF16)],
    )
    return pl.pallas_call(
        _experts_kernel,
        out_shape=jax.ShapeDtypeStruct((P, D), F32),
        grid_spec=grid_spec,
        compiler_params=_cparams(("arbitrary",), EXPERT_VMEM_LIMIT),
        name="experts",
    )(block_a, block_b, nact, xd, by_group(w1), by_group(w3), by_group(w2))


def _combine_kernel(dest_hbm, y_hbm, x_ref, g_ref, b_ref, o_ref, dsm, ybuf, sem_idx, sem_row, *, alpha):
    i = pl.program_id(0)
    n = pl.num_programs(0)
    G = x_ref.shape[0]
    D = o_ref.shape[2]
    NI = G * SUBLANES

    def idx_copy(t):
        s = t % IDX_SLOTS
        return pltpu.make_async_copy(dest_hbm.at[t], dsm.at[pl.ds(s * NI, NI)], sem_idx.at[s])

    def issue_gather(t):
        base = (t % IDX_SLOTS) * NI
        s2 = t % 2

        def body(grp, _):
            at = base + SUBLANES * grp
            for u in range(SUBLANES):
                d = dsm[at + u]
                pltpu.make_async_copy(
                    y_hbm.at[lax.shift_right_logical(d, 3), pl.ds(d & (SUBLANES - 1), 1), :],
                    ybuf.at[s2, grp, pl.ds(u, 1), :], sem_row.at[s2]).start(priority=u % 2)
            return 0

        lax.fori_loop(0, G, body, 0)

    @pl.when(i == 0)
    def _():
        c0 = idx_copy(0)
        c0.start()
        c0.wait()
        issue_gather(0)

        @pl.when(n > 1)
        def _():
            idx_copy(1).start()

    @pl.when(i + 1 < n)
    def _():
        idx_copy(i + 1).wait()
        issue_gather(i + 1)

    @pl.when(i + 2 < n)
    def _():
        idx_copy(i + 2).start()

    s2 = i % 2
    pltpu.make_async_copy(ybuf.at[s2], ybuf.at[s2], sem_row.at[s2]).wait()
    o_ref[...] = _layer_norm_rows(alpha * x_ref[:, :, 0:D] + ybuf[s2], g_ref[...], b_ref[...])


def _combine_call(dest, y, x1e, g, b, alpha):
    T = x1e.shape[0]
    D = y.shape[1]
    R = ROW_TILE
    dest = dest.reshape(T // R, R)
    G = R // SUBLANES
    tiles = lambda a: a.reshape(a.shape[0] // SUBLANES, SUBLANES, a.shape[1])
    rows_spec = lambda lanes: pl.BlockSpec((G, SUBLANES, lanes), lambda i: (i, 0, 0))
    vec_spec = pl.BlockSpec((1, 1, D), lambda i: (0, 0, 0))
    out = pl.pallas_call(
        functools.partial(_combine_kernel, alpha=alpha),
        out_shape=jax.ShapeDtypeStruct((T // SUBLANES, SUBLANES, D), F32),
        grid=(T // R,),
        in_specs=[pl.BlockSpec(memory_space=pl.ANY), pl.BlockSpec(memory_space=pl.ANY),
                  rows_spec(x1e.shape[1]), vec_spec, vec_spec],
        out_specs=rows_spec(D),
        scratch_shapes=[pltpu.SMEM((IDX_SLOTS * R,), jnp.int32),
                        pltpu.VMEM((2, G, SUBLANES, D), F32),
                        pltpu.SemaphoreType.DMA((IDX_SLOTS,)),
                        pltpu.SemaphoreType.DMA((2,))],
        compiler_params=_cparams(("arbitrary",), VMEM_LIMIT),
        name="combine_ln",
    )(dest, tiles(y), tiles(x1e), g.reshape(1, 1, D), b.reshape(1, 1, D))
    return out.reshape(T, D)


def _pair_expert_tables():
    lo_hi = [(lo, hi) for lo in range(EXPERTS_PER_GROUP) for hi in range(lo + 1, EXPERTS_PER_GROUP)]
    ea = [g * EXPERTS_PER_GROUP + lo for g in range(N_EXPERT_GROUPS) for lo, _ in lo_hi]
    eb = [g * EXPERTS_PER_GROUP + hi for g in range(N_EXPERT_GROUPS) for _, hi in lo_hi]
    return jnp.asarray(ea, jnp.int32), jnp.asarray(eb, jnp.int32)


def _dispatch_tables(route, cnt, T):
    RB = EXPERT_ROWS
    pairs = jnp.arange(N_PAIRS, dtype=jnp.int32)
    counts = cnt[0, 0:N_PAIRS].astype(jnp.int32)
    padded = (counts + RB - 1) // RB * RB
    p_ends = jnp.cumsum(padded)
    p_starts = p_ends - padded
    rank = route[:, 4].astype(jnp.int32)
    pair = route[:, 5].astype(jnp.int32)
    start_of = jnp.sum(jnp.where(pair[:, None] == pairs[None, :], p_starts[None, :], 0), axis=-1)
    dest = start_of + rank
    P = -(-T // RB) * RB + N_PAIRS * RB
    n_blocks = P // RB
    blocks = jnp.arange(n_blocks, dtype=jnp.int32)
    block_pair = jnp.minimum(
        jnp.sum((p_ends[None, :] <= (blocks * RB)[:, None]).astype(jnp.int32), axis=1), N_PAIRS - 1)
    ea, eb = _pair_expert_tables()
    pick = block_pair[:, None] == pairs[None, :]
    block_a = jnp.sum(jnp.where(pick, ea[None, :], 0), axis=1)
    block_b = jnp.sum(jnp.where(pick, eb[None, :], 0), axis=1)
    nact = (p_ends[-1] // RB).astype(jnp.int32).reshape(1)
    last_partial = jnp.any((blocks[:, None] == (p_ends // RB - 1)[None, :]) & ((counts % RB) != 0)[None, :], axis=1)
    fill_flag = (last_partial | (blocks >= nact[0])).astype(jnp.int32)
    return dest, block_a, block_b, nact, fill_flag, P


def kernel(x, ln_in_g, ln_in_b, w_in, b_f, conv_w, conv_b, sgu_ln_g, sgu_ln_b, sgu_w, sgu_b, grp_g, w_out, ln1_g, ln1_b, router_g_w, router_g_b, router_e_w, router_e_b, w1, w3, w2, ln2_g, ln2_b):
    B, S, D = x.shape
    depth = w_in.shape[0]
    W = D // N_MIXERS
    T = B * S
    alpha = (2.0 * depth) ** 0.25
    scale = 1.0 / math.sqrt(HEAD_DIM)

    xs = x.reshape(T, D)
    for l in range(depth):
        wl = w_in[l]
        o_f = 6 * W
        o_cv = o_f + HEADS
        o_sg = o_cv + 3 * W
        f_cols = jnp.pad(wl[:, o_f:o_cv], ((0, 0), (0, LANES - HEADS)))
        w_all = jnp.concatenate(
            [wl[:, 0:W] * scale, wl[:, W:3 * W], wl[:, 3 * W:4 * W] * scale, wl[:, 4 * W:6 * W],
             wl[:, o_cv:o_sg], wl[:, o_sg:], f_cols], axis=1).astype(BF16)
        rw = jnp.pad(jnp.concatenate([router_g_w[l], router_e_w[l]], axis=1),
                     ((0, 0), (0, ROUTE_LANES - N_EXPERT_GROUPS - N_EXPERTS)))
        rb = jnp.pad(jnp.concatenate([router_g_b[l], router_e_b[l]]),
                     (0, ROUTE_LANES - N_EXPERT_GROUPS - N_EXPERTS)).reshape(1, ROUTE_LANES)
        gains = [grp_g[l, k * W:(k + 1) * W].reshape(1, W) for k in range(N_MIXERS)]

        if l == 0:
            qkv_a, qkv_b, cv, sg, f_pad, xs = _inproj_call(xs, w_all, W, entry_ln=(ln_in_g, ln_in_b))
        else:
            qkv_a, qkv_b, cv, sg, f_pad = _inproj_call(xs, w_all, W)
        f_rows = f_pad[:, 0:HEADS].reshape(B, S, HEADS).transpose(0, 2, 1).reshape(B * HEADS, S)
        b_rows = jnp.tile(b_f[l], B).reshape(B * HEADS, 1)
        c = _fcum_call(f_rows, b_rows)
        c_tiles = c.reshape(B, HEADS * (S // ATT_K), ATT_K)
        mix_a = _sb_attn_call(qkv_a, gains[0], B, S, W)
        mix_b = _fox_attn_call(qkv_b, c_tiles, gains[1], B, S, W)
        mix_c, mix_d = _conv_sgu_call(cv, sg, conv_w[l], conv_b[l], sgu_ln_g[l], sgu_ln_b[l],
                                      sgu_w[l], sgu_b[l], gains[2][0], gains[3][0], S, W)
        x1e, route, cnt = _outproj_router_call([mix_a, mix_b, mix_c, mix_d], w_out[l].astype(BF16), xs,
                                               ln1_g[l], ln1_b[l], rw, rb, alpha)
        dest, block_a, block_b, nact, fill_flag, n_rows = _dispatch_tables(route, cnt, T)
        xd = _dispatch_call(fill_flag, dest, x1e, n_rows)
        y = _experts_call(block_a, block_b, nact, xd, w1, w3, w2, l)
        xs = _combine_call(dest, y, x1e, ln2_g[l], ln2_b[l], alpha)
    return xs.reshape(B, S, D)
```

```python
import functools
import math

import jax
import jax.numpy as jnp
from jax import lax
from jax.experimental import pallas as pl
from jax.experimental.pallas import tpu as pltpu

N_MIXERS = 4
HEAD_DIM = 64
HEADS = 4
CONV_WIDTH = 3
SGU_BLOCK = 128
CHUNK = 64
N_EXPERT_GROUPS = 4
EXPERTS_PER_GROUP = 8
N_EXPERTS = N_EXPERT_GROUPS * EXPERTS_PER_GROUP
TOP_K = 2
LN_EPS = 1e-5
RMS_EPS = 1e-6

LANES = 128
V7X_VMEM_BYTES = 64 * 1024 * 1024
VMEM_LIMIT = 48 * 1024 * 1024
EXPERT_VMEM_LIMIT = 56 * 1024 * 1024

ROW_TILE = 512
DISPATCH_TILE = 1024
ATT_Q = 256
ATT_K = 256
EXPERT_ROWS = 128
EXPERT_STEP_ROWS = 512
ROUTE_LANES = LANES

F32 = jnp.float32
BF16 = jnp.bfloat16


def _cparams(sem, vmem=None):
    return pltpu.CompilerParams(dimension_semantics=sem, vmem_limit_bytes=vmem)


def _layer_norm_rows(y, g, b):
    mu = jnp.mean(y, axis=-1, keepdims=True)
    d = y - mu
    var = jnp.mean(d * d, axis=-1, keepdims=True)
    return d * lax.rsqrt(var + LN_EPS) * g + b


def _softplus(z):
    return jnp.maximum(z, 0.0) + jnp.log1p(jnp.exp(-jnp.abs(z)))


def _inproj_kernel(x_ref, w_ref, *refs, widths, entry_norm):
    if entry_norm:
        g_ref, b_ref, qa_ref, qb_ref, cv_ref, sg_ref, f_ref, xn_ref = refs
        x = _layer_norm_rows(x_ref[...], g_ref[...], b_ref[...])
        xn_ref[...] = x
    else:
        qa_ref, qb_ref, cv_ref, sg_ref, f_ref = refs
        x = x_ref[...]
    xb = x.astype(BF16)
    off = 0
    for ref, wd in zip((qa_ref, qb_ref, cv_ref, sg_ref, f_ref), widths):
        ref[...] = jnp.dot(xb, w_ref[:, off:off + wd], preferred_element_type=F32).astype(ref.dtype)
        off += wd


def _inproj_call(x, w_all, W, entry_ln=None):
    T, D = x.shape
    widths = (3 * W, 3 * W, 3 * W, 2 * W, LANES)
    NC = sum(widths)
    dts = (BF16, BF16, BF16, BF16, F32)
    row_spec = pl.BlockSpec((ROW_TILE, D), lambda i: (i, 0))
    vec_spec = pl.BlockSpec((1, D), lambda i: (0, 0))
    in_specs = [row_spec, pl.BlockSpec((D, NC), lambda i: (0, 0))]
    out_shape = [jax.ShapeDtypeStruct((T, wd), dt) for wd, dt in zip(widths, dts)]
    out_specs = [pl.BlockSpec((ROW_TILE, wd), lambda i: (i, 0)) for wd in widths]
    args = [x, w_all]
    if entry_ln is not None:
        in_specs += [vec_spec, vec_spec]
        out_shape.append(jax.ShapeDtypeStruct((T, D), F32))
        out_specs.append(row_spec)
        args += [entry_ln[0].reshape(1, D), entry_ln[1].reshape(1, D)]
    return pl.pallas_call(
        functools.partial(_inproj_kernel, widths=widths, entry_norm=entry_ln is not None),
        out_shape=out_shape,
        grid=(T // ROW_TILE,),
        in_specs=in_specs,
        out_specs=out_specs,
        compiler_params=_cparams(("parallel",), VMEM_LIMIT),
        name="inproj",
    )(*args)


def _fcum_kernel(f_ref, b_ref, tri_ref, c_ref):
    R, S = f_ref.shape
    y = f_ref[...] + b_ref[...]
    lf = -_softplus(-y)
    tri = tri_ref[...]
    carry = jnp.zeros((R, 1), F32)
    for blk in range(S // LANES):
        seg = lf[:, blk * LANES:(blk + 1) * LANES]
        s1 = seg.astype(BF16)
        r1 = seg - s1.astype(F32)
        s2 = r1.astype(BF16)
        s3 = (r1 - s2.astype(F32)).astype(BF16)
        cs = (jnp.dot(s1, tri, preferred_element_type=F32)
              + jnp.dot(s2, tri, preferred_element_type=F32)
              + jnp.dot(s3, tri, preferred_element_type=F32)) + carry
        c_ref[:, blk * LANES:(blk + 1) * LANES] = cs
        carry = cs[:, LANES - 1:LANES]


def _fcum_call(f_rows, b_rows):
    R, S = f_rows.shape
    idx = jnp.arange(LANES)
    tri = (idx[:, None] <= idx[None, :]).astype(BF16)
    return pl.pallas_call(
        _fcum_kernel,
        out_shape=jax.ShapeDtypeStruct((R, S), F32),
        name="forget_cumsum",
    )(f_rows, b_rows, tri)


def _rms_group_store(o_ref, out, g_ref):
    ms = jnp.mean(out * out, axis=-1, keepdims=True)
    o_ref[...] = (out * lax.rsqrt(ms + RMS_EPS) * g_ref[...]).astype(o_ref.dtype)


def _head_masks():
    lane = lax.broadcasted_iota(jnp.int32, (1, LANES), 1)
    return [lane < HEAD_DIM, lane >= HEAD_DIM]


def _nt_dot(a, b):
    return lax.dot_general(a, b, (((1,), (1,)), ((), ())), preferred_element_type=F32)


SB_DEAD = -104.0


def _neg_softplus(z):
    return jnp.minimum(-z, 0.0) - jnp.log(1.0 + jnp.exp(-jnp.abs(z)))


def _sb_attn_kernel(q_ref, k_ref, v_ref, m_ref, g_ref, o_ref):
    Q = q_ref.shape[0]
    q0 = pl.program_id(1) * Q
    i = q0 // ATT_K
    row = lax.broadcasted_iota(jnp.int32, (Q, ATT_K), 0) + (q0 - i * ATT_K)
    col = lax.broadcasted_iota(jnp.int32, (Q, ATT_K), 1)
    diag_mask = col < row
    hmasks = _head_masks()

    def tile(qms, j, state, mask):
        ks = pl.multiple_of(j * ATT_K, ATT_K)
        k_ts = [k_ref[pl.ds(ks, ATT_K), p * LANES:(p + 1) * LANES] for p in range(HEADS // 2)]
        v_ts = [v_ref[pl.ds(ks, ATT_K), p * LANES:(p + 1) * LANES] for p in range(HEADS // 2)]
        later_mat = m_ref[...]
        zs = [_nt_dot(qms[h], k_ts[h // 2]) for h in range(HEADS)]
        mid = []
        for h in range(HEADS):
            log_1m = _neg_softplus(zs[h])
            log_b = zs[h] + log_1m
            if mask is not None:
                log_1m = jnp.where(mask, log_1m, 0.0)
            hi = log_1m.astype(BF16)
            lo = (log_1m - hi.astype(F32)).astype(BF16)
            later = (jnp.dot(hi, later_mat, preferred_element_type=F32)
                     + jnp.dot(lo, later_mat, preferred_element_type=F32)) + state[2 * h]
            mid.append((log_b, later, log_1m[:, 0:1]))
        out = []
        for h in range(HEADS):
            log_b, later, first = mid[h]
            w = jnp.exp(log_b + later)
            if mask is not None:
                w = jnp.where(mask, w, 0.0)
            acc = state[2 * h + 1] + jnp.dot(w.astype(BF16), v_ts[h // 2], preferred_element_type=F32)
            out.extend((later[:, 0:1] + first, acc))
        return tuple(out)

    qms = []
    for h in range(HEADS):
        q_pair = q_ref[:, (h // 2) * LANES:(h // 2 + 1) * LANES]
        qms.append(jnp.where(hmasks[h % 2], q_pair, jnp.zeros_like(q_pair)))
    def alive(state):
        top = state[0]
        for h in range(1, HEADS):
            top = jnp.maximum(top, state[2 * h])
        return (jnp.max(top) >= SB_DEAD).astype(jnp.int32)

    state = (jnp.zeros((Q, 1), F32), jnp.zeros((Q, LANES), F32)) * HEADS
    state = tile(qms, i, state, diag_mask)

    def cond(c):
        return (c[0] <= i) & (c[1] > 0)

    def body(c):
        st = tile(qms, i - c[0], tuple(c[2:]), None)
        return (c[0] + 1, alive(st)) + st

    state = lax.while_loop(cond, body, (jnp.int32(1), alive(state)) + state)[2:]
    outs = [jnp.where(hmasks[0], state[4 * p + 1], state[4 * p + 3]) for p in range(HEADS // 2)]
    _rms_group_store(o_ref, jnp.concatenate(outs, axis=-1), g_ref)


def _fox_attn_kernel(q_ref, k_ref, v_ref, c_ref, g_ref, o_ref):
    Q = q_ref.shape[0]
    q0 = pl.program_id(1) * Q
    i = q0 // ATT_K
    nk = c_ref.shape[1] // HEADS
    row = lax.broadcasted_iota(jnp.int32, (Q, ATT_K), 0) + (q0 - i * ATT_K)
    col = lax.broadcasted_iota(jnp.int32, (Q, ATT_K), 1)
    diag_mask = col <= row
    hmasks = _head_masks()
    neg = jnp.finfo(F32).min

    def tile(qms, j, state, mask):
        ks = pl.multiple_of(j * ATT_K, ATT_K)
        k_ts = [k_ref[pl.ds(ks, ATT_K), p * LANES:(p + 1) * LANES] for p in range(HEADS // 2)]
        v_ts = [v_ref[pl.ds(ks, ATT_K), p * LANES:(p + 1) * LANES] for p in range(HEADS // 2)]
        ss = []
        for h in range(HEADS):
            c_row = c_ref[0, pl.ds(h * nk + j, 1), :]
            s = _nt_dot(qms[h], k_ts[h // 2]) - c_row
            if mask is not None:
                s = jnp.where(mask, s, neg)
            ss.append(s)
        mid = []
        for h in range(HEADS):
            m, l = state[3 * h], state[3 * h + 1]
            m_new = jnp.maximum(m, jnp.max(ss[h], axis=-1, keepdims=True))
            alpha = jnp.exp(m - m_new)
            p = jnp.exp(ss[h] - m_new)
            l = alpha * l + jnp.sum(p, axis=-1, keepdims=True)
            mid.append((m_new, l, alpha, p.astype(BF16)))
        out = []
        for h in range(HEADS):
            m_new, l, alpha, p = mid[h]
            acc = alpha * state[3 * h + 2] + jnp.dot(p, v_ts[h // 2], preferred_element_type=F32)
            out.extend((m_new, l, acc))
        return tuple(out)

    qms = []
    for h in range(HEADS):
        q_pair = q_ref[:, (h // 2) * LANES:(h // 2 + 1) * LANES]
        qms.append(jnp.where(hmasks[h % 2], q_pair, jnp.zeros_like(q_pair)))
    state = (jnp.full((Q, 1), neg, F32), jnp.zeros((Q, 1), F32), jnp.zeros((Q, LANES), F32)) * HEADS
    state = tile(qms, i, state, diag_mask)
    state = lax.fori_loop(1, i + 1, lambda jj, st: tile(qms, i - jj, st, None), state)
    heads = [state[3 * h + 2] / state[3 * h + 1] for h in range(HEADS)]
    outs = [jnp.where(hmasks[0], heads[2 * p], heads[2 * p + 1]) for p in range(HEADS // 2)]
    _rms_group_store(o_ref, jnp.concatenate(outs, axis=-1), g_ref)


def _attn_specs(B, S, W):
    nq = S // ATT_Q
    q_spec = pl.BlockSpec((ATT_Q, W), lambda b, i: (b * nq + i, 0))
    k_spec = pl.BlockSpec((S, W), lambda b, i: (b, 1))
    v_spec = pl.BlockSpec((S, W), lambda b, i: (b, 2))
    g_spec = pl.BlockSpec((1, W), lambda b, i: (0, 0))
    o_spec = pl.BlockSpec((ATT_Q, W), lambda b, i: (b * nq + i, 0))
    return nq, q_spec, k_spec, v_spec, g_spec, o_spec


def _sb_attn_call(qkv, gain, B, S, W):
    T = B * S
    nq, q_spec, k_spec, v_spec, g_spec, o_spec = _attn_specs(B, S, W)
    idx = jnp.arange(ATT_K)
    later_mat = (idx[:, None] > idx[None, :]).astype(BF16)
    return pl.pallas_call(
        _sb_attn_kernel,
        out_shape=jax.ShapeDtypeStruct((T, W), BF16),
        grid=(B, nq),
        in_specs=[q_spec, k_spec, v_spec,
                  pl.BlockSpec((ATT_K, ATT_K), lambda b, i: (0, 0)), g_spec],
        out_specs=o_spec,
        compiler_params=_cparams(("parallel", "arbitrary"), VMEM_LIMIT),
        name="sb_attn",
    )(qkv, qkv, qkv, later_mat, gain)


def _fox_attn_call(qkv, c_tiles, gain, B, S, W):
    T = B * S
    nq, q_spec, k_spec, v_spec, g_spec, o_spec = _attn_specs(B, S, W)
    rows = c_tiles.shape[1]
    return pl.pallas_call(
        _fox_attn_kernel,
        out_shape=jax.ShapeDtypeStruct((T, W), BF16),
        grid=(B, nq),
        in_specs=[q_spec, k_spec, v_spec,
                  pl.BlockSpec((1, rows, ATT_K), lambda b, i: (b, 0, 0)), g_spec],
        out_specs=o_spec,
        compiler_params=_cparams(("parallel", "arbitrary"), VMEM_LIMIT),
        name="fox_attn",
    )(qkv, qkv, qkv, c_tiles, gain)


HALO = 16


def _gelu_tanh(x):
    c = math.sqrt(2.0 / math.pi)
    return 0.5 * x * (1.0 + jnp.tanh(c * (x + 0.044715 * (x * x * x))))


def _conv_sgu_kernel(cv_ref, halo_ref, sg_ref, cw_ref, cb_ref, lg_ref, lb_ref, ws_ref, bs_ref,
                     gc_ref, gd_ref, oc_ref, od_ref, *, tiles_per_seq):
    R, W3 = cv_ref.shape
    W = W3 // 3
    i = pl.program_id(0)
    z = cv_ref[:, 2 * W:3 * W].astype(F32) * cv_ref[:, 0:W].astype(F32)
    zh = halo_ref[:, 2 * W:3 * W].astype(F32) * halo_ref[:, 0:W].astype(F32)
    zh = jnp.where(i % tiles_per_seq == 0, jnp.zeros_like(zh), zh)
    zf = jnp.concatenate([zh, z], axis=0)
    z1 = pltpu.roll(zf, 1, 0)[HALO:]
    z2 = pltpu.roll(zf, 2, 0)[HALO:]
    y = cb_ref[...] + cw_ref[0:1, :] * z2
    y = y + cw_ref[1:2, :] * z1
    y = y + cw_ref[2:3, :] * z
    _rms_group_store(oc_ref, cv_ref[:, W:2 * W].astype(F32) * y, gc_ref)
    gel = _gelu_tanh(sg_ref[...].astype(F32))
    u = gel[:, 0:W]
    vn = _layer_norm_rows(gel[:, W:2 * W], lg_ref[...], lb_ref[...]).astype(BF16)
    lane = lax.broadcasted_iota(jnp.int32, (1, W), 1)
    pi = lax.broadcasted_iota(jnp.int32, (SGU_BLOCK, SGU_BLOCK), 0) // CHUNK
    pj = lax.broadcasted_iota(jnp.int32, (SGU_BLOCK, SGU_BLOCK), 1) // CHUNK
    w_m = [jnp.where(pj <= pi, ws_ref[g], 0.0).astype(BF16) for g in range(HEADS)]
    blocks = []
    for n in range(R // SGU_BLOCK):
        vb = vn[n * SGU_BLOCK:(n + 1) * SGU_BLOCK, :]
        mixed = jnp.dot(w_m[0], vb, preferred_element_type=F32)
        for g in range(1, HEADS):
            mg = jnp.dot(w_m[g], vb, preferred_element_type=F32)
            mixed = jnp.where(lane >= g * HEAD_DIM, mg, mixed)
        blocks.append(mixed + bs_ref[...])
    mixed = jnp.concatenate(blocks, axis=0)
    _rms_group_store(od_ref, u * mixed, gd_ref)


def _conv_sgu_call(cv, sg, conv_w, conv_b, ln_g, ln_b, sgu_w, sgu_b, g_c, g_d, S, W):
    T = cv.shape[0]
    R = ROW_TILE
    tiles_per_seq = S // R
    bias_tile = jnp.repeat(sgu_b.T, HEAD_DIM, axis=1)
    row = lambda a: a.reshape(1, W)
    const = lambda shape: pl.BlockSpec(shape, lambda i: tuple(0 for _ in shape))
    return pl.pallas_call(
        functools.partial(_conv_sgu_kernel, tiles_per_seq=tiles_per_seq),
        out_shape=[jax.ShapeDtypeStruct((T, W), BF16)] * 2,
        grid=(T // R,),
        in_specs=[pl.BlockSpec((R, 3 * W), lambda i: (i, 0)),
                  pl.BlockSpec((HALO, 3 * W), lambda i: (jnp.maximum(i * (R // HALO) - 1, 0), 0)),
                  pl.BlockSpec((R, 2 * W), lambda i: (i, 0)),
                  const((CONV_WIDTH, W)), const((1, W)), const((1, W)), const((1, W)),
                  const((HEADS, SGU_BLOCK, SGU_BLOCK)), const((SGU_BLOCK, W)),
                  const((1, W)), const((1, W))],
        out_specs=[pl.BlockSpec((R, W), lambda i: (i, 0))] * 2,
        compiler_params=_cparams(("parallel",), VMEM_LIMIT),
        name="conv_sgu",
    )(cv, cv, sg, conv_w, row(conv_b), row(ln_g), row(ln_b), sgu_w, bias_tile, row(g_c), row(g_d))


ROUTE_CHUNK = ROW_TILE
PAIRS_PER_GROUP = EXPERTS_PER_GROUP * (EXPERTS_PER_GROUP - 1) // 2
N_PAIRS = N_EXPERT_GROUPS * PAIRS_PER_GROUP


def _outproj_router_kernel(ma_ref, mb_ref, mc_ref, md_ref, wo_ref, x_ref, g_ref, b_ref,
                           rwa_ref, rwh_ref, rb_ref, tri_ref, x1_ref, route_ref, cnt_ref, base_ref, *, alpha):
    @pl.when(pl.program_id(0) == 0)
    def _():
        base_ref[...] = jnp.zeros(base_ref.shape, base_ref.dtype)

    base = base_ref[...]
    for c in range(x_ref.shape[0] // ROUTE_CHUNK):
        rows = slice(c * ROUTE_CHUNK, (c + 1) * ROUTE_CHUNK)
        base = _outproj_route_rows(rows, ma_ref, mb_ref, mc_ref, md_ref, wo_ref, x_ref, g_ref, b_ref,
                                   rwa_ref, rwh_ref, rb_ref, tri_ref, x1_ref, route_ref, base, alpha)
    base_ref[...] = base
    cnt_ref[...] = jnp.broadcast_to(base, cnt_ref.shape)


def _outproj_route_rows(rows, ma_ref, mb_ref, mc_ref, md_ref, wo_ref, x_ref, g_ref, b_ref,
                        rwa_ref, rwh_ref, rb_ref, tri_ref, x1_ref, route_ref, base, alpha):
    W = ma_ref.shape[1]
    acc = jnp.dot(ma_ref[rows, :], wo_ref[0:W, :], preferred_element_type=F32)
    for k, ref in enumerate((mb_ref, mc_ref, md_ref), start=1):
        acc = acc + jnp.dot(ref[rows, :], wo_ref[k * W:(k + 1) * W, :], preferred_element_type=F32)
    x1 = _layer_norm_rows(alpha * x_ref[rows, :] + acc, g_ref[...], b_ref[...])
    D = x1.shape[1]
    x1_ref[rows, 0:D] = x1
    xh = x1.astype(BF16)
    xl = (x1 - xh.astype(F32)).astype(BF16)
    both = jnp.dot(xh, rwa_ref[...], preferred_element_type=F32)
    logits = (both[:, 0:ROUTE_LANES] + both[:, ROUTE_LANES:2 * ROUTE_LANES]
              + jnp.dot(xl, rwh_ref[...], preferred_element_type=F32)) + rb_ref[...]
    R = logits.shape[0]
    lane = lax.broadcasted_iota(jnp.int32, (R, ROUTE_LANES), 1).astype(F32)
    big = float(ROUTE_LANES)
    ninf = jnp.finfo(F32).min
    gmask = lane < N_EXPERT_GROUPS
    lg = jnp.where(gmask, logits, ninf)
    mg = jnp.max(lg, axis=-1, keepdims=True)
    sum_g = jnp.sum(jnp.where(gmask, jnp.exp(lg - mg), 0.0), axis=-1, keepdims=True)
    p_gsel = 1.0 / sum_g
    g_sel = jnp.min(jnp.where(lg == mg, lane, big), axis=-1, keepdims=True)
    e_lo = N_EXPERT_GROUPS + EXPERTS_PER_GROUP * g_sel
    emask = (lane >= e_lo) & (lane < e_lo + EXPERTS_PER_GROUP)
    le = jnp.where(emask, logits, ninf)
    me = jnp.max(le, axis=-1, keepdims=True)
    ee = jnp.where(emask, jnp.exp(le - me), 0.0)
    pe = ee / jnp.sum(ee, axis=-1, keepdims=True)
    pe = jnp.where(emask, pe, -1.0)
    p1 = jnp.max(pe, axis=-1, keepdims=True)
    i1 = jnp.min(jnp.where(pe == p1, lane, big), axis=-1, keepdims=True)
    pe2 = jnp.where(lane == i1, -1.0, pe)
    p2 = jnp.max(pe2, axis=-1, keepdims=True)
    i2 = jnp.min(jnp.where(pe2 == p2, lane, big), axis=-1, keepdims=True)
    denom = p1 + p2
    gate1 = p_gsel * p1 / denom
    gate2 = p_gsel * p2 / denom
    a1 = i1 - e_lo
    a2 = i2 - e_lo
    lo = jnp.minimum(a1, a2)
    hi = jnp.maximum(a1, a2)
    first_low = a1 < a2
    gate_lo = jnp.where(first_low, gate1, gate2)
    gate_hi = jnp.where(first_low, gate2, gate1)
    pair = g_sel * PAIRS_PER_GROUP + lo * ((2 * EXPERTS_PER_GROUP - 1) - lo) * 0.5 + (hi - lo - 1.0)
    ex_lo = g_sel * EXPERTS_PER_GROUP + lo
    ex_hi = g_sel * EXPERTS_PER_GROUP + hi
    sel = lane == pair
    one = jnp.where(sel, 1.0, 0.0)
    before = jnp.dot(tri_ref[...], one.astype(BF16), preferred_element_type=F32) + base
    rank = jnp.sum(jnp.where(sel, before, 0.0), axis=-1, keepdims=True)
    route = jnp.where(lane == 0, ex_lo, jnp.where(lane == 1, ex_hi, jnp.where(lane == 2, gate_lo,
            jnp.where(lane == 3, gate_hi, jnp.where(lane == 4, rank, pair)))))
    route_ref[rows, :] = route
    x1_ref[rows, D:D + ROUTE_LANES] = route
    return before[R - 1:R, :] + one[R - 1:R, :]


def _outproj_router_call(mixes, w_out, x, g, b, rw, rb, alpha):
    T, D = x.shape
    W = mixes[0].shape[1]
    R = ROUTE_CHUNK
    rw_hi = rw.astype(BF16)
    rw_lo = (rw - rw_hi.astype(F32)).astype(BF16)
    rwa = jnp.concatenate([rw_hi, rw_lo], axis=1)
    idx = jnp.arange(ROUTE_CHUNK)
    tri = (idx[None, :] < idx[:, None]).astype(BF16)
    const = lambda shape: pl.BlockSpec(shape, lambda i: tuple(0 for _ in shape))
    return pl.pallas_call(
        functools.partial(_outproj_router_kernel, alpha=alpha),
        out_shape=[jax.ShapeDtypeStruct((T, D + ROUTE_LANES), F32), jax.ShapeDtypeStruct((T, ROUTE_LANES), F32),
                   jax.ShapeDtypeStruct((8, ROUTE_LANES), F32)],
        grid=(T // R,),
        in_specs=[pl.BlockSpec((R, W), lambda i: (i, 0))] * 4
        + [const((D, D)), pl.BlockSpec((R, D), lambda i: (i, 0)), const((1, D)), const((1, D)),
           const((D, 2 * ROUTE_LANES)), const((D, ROUTE_LANES)), const((1, ROUTE_LANES)),
           const((ROUTE_CHUNK, ROUTE_CHUNK))],
        out_specs=[pl.BlockSpec((R, D + ROUTE_LANES), lambda i: (i, 0)),
                   pl.BlockSpec((R, ROUTE_LANES), lambda i: (i, 0)),
                   const((8, ROUTE_LANES))],
        scratch_shapes=[pltpu.VMEM((1, ROUTE_LANES), F32)],
        compiler_params=_cparams(("arbitrary",), VMEM_LIMIT),
        name="outproj_router",
    )(*mixes, w_out, x, g.reshape(1, D), b.reshape(1, D), rwa, rw_hi, rb, tri)


IDX_SLOTS = 3
SUBLANES = 8


def _dispatch_kernel(fill_ref, dest_hbm, x_ref, xd_hbm, dsm, zbuf, sem_idx, sem_row, sem_fill):
    i = pl.program_id(0)
    n = pl.num_programs(0)
    R = x_ref.shape[0] * SUBLANES
    TB = zbuf.shape[0]
    NI = R

    def idx_copy(t):
        s = t % 2
        return pltpu.make_async_copy(dest_hbm.at[t], dsm.at[pl.ds(s * NI, NI)], sem_idx.at[s])

    def fill_copy(blk):
        return pltpu.make_async_copy(zbuf, xd_hbm.at[pl.ds(blk * TB, TB)], sem_fill)

    @pl.when(i == 0)
    def _():
        idx_copy(0).start()
        zbuf[...] = jnp.zeros(zbuf.shape, zbuf.dtype)

        def start_body(blk, _):
            @pl.when(fill_ref[blk] != 0)
            def _():
                fill_copy(blk).start()
            return 0

        def wait_body(blk, _):
            @pl.when(fill_ref[blk] != 0)
            def _():
                fill_copy(blk).wait()
            return 0

        lax.fori_loop(0, fill_ref.shape[0], start_body, 0)
        lax.fori_loop(0, fill_ref.shape[0], wait_body, 0)

    idx_copy(i).wait()

    @pl.when(i + 1 < n)
    def _():
        idx_copy(i + 1).start()

    base = (i % 2) * NI

    def body(grp, _):
        at = base + SUBLANES * grp
        for u in range(SUBLANES):
            d = dsm[at + u]
            pltpu.make_async_copy(
                x_ref.at[grp, pl.ds(u, 1), :],
                xd_hbm.at[lax.shift_right_logical(d, 3), pl.ds(d & (SUBLANES - 1), 1), :],
                sem_row).start(priority=u % 2)
        return 0

    lax.fori_loop(0, R // SUBLANES, body, 0)
    pltpu.make_async_copy(x_ref, x_ref, sem_row).wait()


def _dispatch_call(fill_flag, dest, x1, n_rows):
    T, D = x1.shape
    R = DISPATCH_TILE
    dest = dest.reshape(T // R, R)
    grid_spec = pltpu.PrefetchScalarGridSpec(
        num_scalar_prefetch=1,
        grid=(T // R,),
        in_specs=[pl.BlockSpec(memory_space=pl.ANY),
                  pl.BlockSpec((R // SUBLANES, SUBLANES, D), lambda i, ff: (i, 0, 0))],
        out_specs=pl.BlockSpec(memory_space=pl.ANY),
        scratch_shapes=[pltpu.SMEM((2 * R,), jnp.int32),
                        pltpu.VMEM((EXPERT_ROWS // SUBLANES, SUBLANES, D), F32),
                        pltpu.SemaphoreType.DMA((2,)),
                        pltpu.SemaphoreType.DMA,
                        pltpu.SemaphoreType.DMA],
    )
    xd = pl.pallas_call(
        _dispatch_kernel,
        out_shape=jax.ShapeDtypeStruct((n_rows // SUBLANES, SUBLANES, D), F32),
        grid_spec=grid_spec,
        compiler_params=_cparams(("arbitrary",), VMEM_LIMIT),
        name="moe_dispatch",
    )(fill_flag, dest, x1.reshape(T // SUBLANES, SUBLANES, D))
    return xd.reshape(n_rows, D)


def _experts_kernel(ea_ref, eb_ref, act_ref, x_ref, w1_ref, w3_ref, w2_ref, y_ref, wb1, wb3, wb2):
    i = pl.program_id(0)
    D = y_ref.shape[1]
    RB = EXPERT_ROWS
    per_step = x_ref.shape[0] // RB
    first = i * per_step
    group = ea_ref[first] // EXPERTS_PER_GROUP
    prev = ea_ref[jnp.maximum(first, 1) - 1] // EXPERTS_PER_GROUP

    @pl.when((i == 0) | (group != prev))
    def _():
        for e in range(EXPERTS_PER_GROUP):
            wb1[e] = w1_ref[0, e].astype(BF16)
            wb3[e] = w3_ref[0, e].astype(BF16)
            wb2[e] = w2_ref[0, e].astype(BF16)

    def expert(xb, e):
        h1 = jnp.dot(xb, wb1[e], preferred_element_type=F32)
        h3 = jnp.dot(xb, wb3[e], preferred_element_type=F32)
        h = (h1 * jax.nn.sigmoid(h1)) * h3
        return jnp.dot(h.astype(BF16), wb2[e], preferred_element_type=F32)

    for s in range(per_step):
        blk = first + s
        rows = slice(s * RB, (s + 1) * RB)

        @pl.when(act_ref[blk] != 0)
        def _(blk=blk, rows=rows):
            xb = x_ref[rows, 0:D].astype(BF16)
            ya = expert(xb, ea_ref[blk] % EXPERTS_PER_GROUP) * x_ref[rows, D + 2:D + 3]
            yb = expert(xb, eb_ref[blk] % EXPERTS_PER_GROUP) * x_ref[rows, D + 3:D + 4]
            y_ref[rows, :] = ya + yb

        @pl.when(act_ref[blk] == 0)
        def _(rows=rows):
            y_ref[rows, :] = jnp.zeros((RB, D), y_ref.dtype)


def _experts_call(block_a, block_b, block_active, xd, w1, w3, w2, layer):
    P, DX = xd.shape
    D = DX - ROUTE_LANES
    SR = EXPERT_STEP_ROWS
    per_step = SR // EXPERT_ROWS
    DE = w1.shape[3]
    EG = EXPERTS_PER_GROUP
    by_group = lambda w: w.reshape(w.shape[0] * N_EXPERT_GROUPS, EG, w.shape[2], w.shape[3])
    wmap = lambda i, ea, eb, act: (layer * N_EXPERT_GROUPS + ea[i * per_step] // EG, 0, 0, 0)
    once = pl.Buffered(1)
    grid_spec = pltpu.PrefetchScalarGridSpec(
        num_scalar_prefetch=3,
        grid=(P // SR,),
        in_specs=[pl.BlockSpec((SR, DX), lambda i, ea, eb, act: (i, 0)),
                  pl.BlockSpec((1, EG, D, DE), wmap, pipeline_mode=once),
                  pl.BlockSpec((1, EG, D, DE), wmap, pipeline_mode=once),
                  pl.BlockSpec((1, EG, DE, D), wmap, pipeline_mode=once)],
        out_specs=pl.BlockSpec((SR, D), lambda i, ea, eb, act: (i, 0)),
        scratch_shapes=[pltpu.VMEM((EG, D, DE), BF16), pltpu.VMEM((EG, D, DE), BF16),
                        pltpu.VMEM((EG, DE, D), BF16)],
    )
    return pl.pallas_call(
        _experts_kernel,
        out_shape=jax.ShapeDtypeStruct((P, D), F32),
        grid_spec=grid_spec,
        compiler_params=_cparams(("arbitrary",), EXPERT_VMEM_LIMIT),
        name="experts",
    )(block_a, block_b, block_active, xd, by_group(w1), by_group(w3), by_group(w2))


def _combine_kernel(dest_hbm, y_hbm, x_ref, g_ref, b_ref, o_ref, dsm, ybuf, sem_idx, sem_row, *, alpha):
    i = pl.program_id(0)
    n = pl.num_programs(0)
    G = x_ref.shape[0]
    D = o_ref.shape[2]
    NI = G * SUBLANES

    def idx_copy(t):
        s = t % IDX_SLOTS
        return pltpu.make_async_copy(dest_hbm.at[t], dsm.at[pl.ds(s * NI, NI)], sem_idx.at[s])

    def issue_gather(t):
        base = (t % IDX_SLOTS) * NI
        s2 = t % 2

        def body(grp, _):
            at = base + SUBLANES * grp
            for u in range(SUBLANES):
                d = dsm[at + u]
                pltpu.make_async_copy(
                    y_hbm.at[lax.shift_right_logical(d, 3), pl.ds(d & (SUBLANES - 1), 1), :],
                    ybuf.at[s2, grp, pl.ds(u, 1), :], sem_row.at[s2]).start(priority=u % 2)
            return 0

        lax.fori_loop(0, G, body, 0)

    @pl.when(i == 0)
    def _():
        c0 = idx_copy(0)
        c0.start()
        c0.wait()
        issue_gather(0)

        @pl.when(n > 1)
        def _():
            idx_copy(1).start()

    @pl.when(i + 1 < n)
    def _():
        idx_copy(i + 1).wait()
        issue_gather(i + 1)

    @pl.when(i + 2 < n)
    def _():
        idx_copy(i + 2).start()

    s2 = i % 2
    pltpu.make_async_copy(ybuf.at[s2], ybuf.at[s2], sem_row.at[s2]).wait()
    o_ref[...] = _layer_norm_rows(alpha * x_ref[:, :, 0:D] + ybuf[s2], g_ref[...], b_ref[...])


def _combine_call(dest, y, x1e, g, b, alpha):
    T = x1e.shape[0]
    D = y.shape[1]
    R = ROW_TILE
    dest = dest.reshape(T // R, R)
    G = R // SUBLANES
    tiles = lambda a: a.reshape(a.shape[0] // SUBLANES, SUBLANES, a.shape[1])
    rows_spec = lambda lanes: pl.BlockSpec((G, SUBLANES, lanes), lambda i: (i, 0, 0))
    vec_spec = pl.BlockSpec((1, 1, D), lambda i: (0, 0, 0))
    out = pl.pallas_call(
        functools.partial(_combine_kernel, alpha=alpha),
        out_shape=jax.ShapeDtypeStruct((T // SUBLANES, SUBLANES, D), F32),
        grid=(T // R,),
        in_specs=[pl.BlockSpec(memory_space=pl.ANY), pl.BlockSpec(memory_space=pl.ANY),
                  rows_spec(x1e.shape[1]), vec_spec, vec_spec],
        out_specs=rows_spec(D),
        scratch_shapes=[pltpu.SMEM((IDX_SLOTS * R,), jnp.int32),
                        pltpu.VMEM((2, G, SUBLANES, D), F32),
                        pltpu.SemaphoreType.DMA((IDX_SLOTS,)),
                        pltpu.SemaphoreType.DMA((2,))],
        compiler_params=_cparams(("arbitrary",), VMEM_LIMIT),
        name="combine_ln",
    )(dest, tiles(y), tiles(x1e), g.reshape(1, 1, D), b.reshape(1, 1, D))
    return out.reshape(T, D)


def _pair_expert_tables():
    lo_hi = [(lo, hi) for lo in range(EXPERTS_PER_GROUP) for hi in range(lo + 1, EXPERTS_PER_GROUP)]
    ea = [g * EXPERTS_PER_GROUP + lo for g in range(N_EXPERT_GROUPS) for lo, _ in lo_hi]
    eb = [g * EXPERTS_PER_GROUP + hi for g in range(N_EXPERT_GROUPS) for _, hi in lo_hi]
    return jnp.asarray(ea, jnp.int32), jnp.asarray(eb, jnp.int32)


def _dispatch_tables(route, cnt, T):
    RB = EXPERT_ROWS
    SR = EXPERT_STEP_ROWS
    NG, PG = N_EXPERT_GROUPS, PAIRS_PER_GROUP
    pairs = jnp.arange(N_PAIRS, dtype=jnp.int32)
    counts = cnt[0, 0:N_PAIRS].astype(jnp.int32)
    padded = ((counts + RB - 1) // RB * RB).reshape(NG, PG)
    in_group_end = jnp.cumsum(padded, axis=1)
    group_rows = (in_group_end[:, -1] + SR - 1) // SR * SR
    group_end = jnp.cumsum(group_rows)
    group_start = group_end - group_rows
    p_starts = (group_start[:, None] + in_group_end - padded).reshape(N_PAIRS)
    p_ends = p_starts + padded.reshape(N_PAIRS)
    rank = route[:, 4].astype(jnp.int32)
    pair = route[:, 5].astype(jnp.int32)
    start_of = jnp.sum(jnp.where(pair[:, None] == pairs[None, :], p_starts[None, :], 0), axis=-1)
    dest = start_of + rank
    P = -(-(T + N_PAIRS * RB + NG * SR) // SR) * SR
    n_blocks = P // RB
    row0 = jnp.arange(n_blocks, dtype=jnp.int32) * RB
    inside = (row0[:, None] >= p_starts[None, :]) & (row0[:, None] < p_ends[None, :])
    ea, eb = _pair_expert_tables()
    block_group = jnp.minimum(jnp.sum((group_end[None, :] <= row0[:, None]).astype(jnp.int32), axis=1), NG - 1)
    block_active = jnp.any(inside, axis=1)
    block_a = jnp.where(block_active, jnp.sum(jnp.where(inside, ea[None, :], 0), axis=1),
                        block_group * EXPERTS_PER_GROUP)
    block_b = jnp.where(block_active, jnp.sum(jnp.where(inside, eb[None, :], 0), axis=1),
                        block_group * EXPERTS_PER_GROUP)
    last_partial = jnp.any((row0[:, None] == (p_ends - RB)[None, :]) & ((counts % RB) != 0)[None, :], axis=1)
    fill_flag = (last_partial | ~block_active).astype(jnp.int32)
    return dest, block_a, block_b, block_active.astype(jnp.int32), fill_flag, P


def kernel(x, ln_in_g, ln_in_b, w_in, b_f, conv_w, conv_b, sgu_ln_g, sgu_ln_b, sgu_w, sgu_b, grp_g, w_out, ln1_g, ln1_b, router_g_w, router_g_b, router_e_w, router_e_b, w1, w3, w2, ln2_g, ln2_b):
    B, S, D = x.shape
    depth = w_in.shape[0]
    W = D // N_MIXERS
    T = B * S
    alpha = (2.0 * depth) ** 0.25
    scale = 1.0 / math.sqrt(HEAD_DIM)

    xs = x.reshape(T, D)
    for l in range(depth):
        wl = w_in[l]
        o_f = 6 * W
        o_cv = o_f + HEADS
        o_sg = o_cv + 3 * W
        f_cols = jnp.pad(wl[:, o_f:o_cv], ((0, 0), (0, LANES - HEADS)))
        w_all = jnp.concatenate(
            [wl[:, 0:W] * scale, wl[:, W:3 * W], wl[:, 3 * W:4 * W] * scale, wl[:, 4 * W:6 * W],
             wl[:, o_cv:o_sg], wl[:, o_sg:], f_cols], axis=1).astype(BF16)
        rw = jnp.pad(jnp.concatenate([router_g_w[l], router_e_w[l]], axis=1),
                     ((0, 0), (0, ROUTE_LANES - N_EXPERT_GROUPS - N_EXPERTS)))
        rb = jnp.pad(jnp.concatenate([router_g_b[l], router_e_b[l]]),
                     (0, ROUTE_LANES - N_EXPERT_GROUPS - N_EXPERTS)).reshape(1, ROUTE_LANES)
        gains = [grp_g[l, k * W:(k + 1) * W].reshape(1, W) for k in range(N_MIXERS)]

        if l == 0:
            qkv_a, qkv_b, cv, sg, f_pad, xs = _inproj_call(xs, w_all, W, entry_ln=(ln_in_g, ln_in_b))
        else:
            qkv_a, qkv_b, cv, sg, f_pad = _inproj_call(xs, w_all, W)
        f_rows = f_pad[:, 0:HEADS].reshape(B, S, HEADS).transpose(0, 2, 1).reshape(B * HEADS, S)
        b_rows = jnp.tile(b_f[l], B).reshape(B * HEADS, 1)
        c = _fcum_call(f_rows, b_rows)
        c_tiles = c.reshape(B, HEADS * (S // ATT_K), ATT_K)
        mix_a = _sb_attn_call(qkv_a, gains[0], B, S, W)
        mix_b = _fox_attn_call(qkv_b, c_tiles, gains[1], B, S, W)
        mix_c, mix_d = _conv_sgu_call(cv, sg, conv_w[l], conv_b[l], sgu_ln_g[l], sgu_ln_b[l],
                                      sgu_w[l], sgu_b[l], gains[2][0], gains[3][0], S, W)
        x1e, route, cnt = _outproj_router_call([mix_a, mix_b, mix_c, mix_d], w_out[l].astype(BF16), xs,
                                               ln1_g[l], ln1_b[l], rw, rb, alpha)
        dest, block_a, block_b, block_active, fill_flag, n_rows = _dispatch_tables(route, cnt, T)
        xd = _dispatch_call(fill_flag, dest, x1e, n_rows)
        y = _experts_call(block_a, block_b, block_active, xd, w1, w3, w2, l)
        xs = _combine_call(dest, y, x1e, ln2_g[l], ln2_b[l], alpha)
    return xs.reshape(B, S, D)
```

```python
import functools
import math

import jax
import jax.numpy as jnp
from jax import lax
from jax.experimental import pallas as pl
from jax.experimental.pallas import tpu as pltpu

N_MIXERS = 4
HEAD_DIM = 64
HEADS = 4
CONV_WIDTH = 3
SGU_BLOCK = 128
CHUNK = 64
N_EXPERT_GROUPS = 4
EXPERTS_PER_GROUP = 8
N_EXPERTS = N_EXPERT_GROUPS * EXPERTS_PER_GROUP
TOP_K = 2
LN_EPS = 1e-5
RMS_EPS = 1e-6

LANES = 128
V7X_VMEM_BYTES = 64 * 1024 * 1024
VMEM_LIMIT = 48 * 1024 * 1024
EXPERT_VMEM_LIMIT = 56 * 1024 * 1024

ROW_TILE = 512
DISPATCH_TILE = 2048
ATT_Q = 256
ATT_K = 256
EXPERT_ROWS = 128
EXPERT_STEP_ROWS = 512
ROUTE_LANES = LANES

F32 = jnp.float32
BF16 = jnp.bfloat16


def _cparams(sem, vmem=None):
    return pltpu.CompilerParams(dimension_semantics=sem, vmem_limit_bytes=vmem)


def _layer_norm_rows(y, g, b):
    mu = jnp.mean(y, axis=-1, keepdims=True)
    d = y - mu
    var = jnp.mean(d * d, axis=-1, keepdims=True)
    return d * lax.rsqrt(var + LN_EPS) * g + b


def _softplus(z):
    return jnp.maximum(z, 0.0) + jnp.log1p(jnp.exp(-jnp.abs(z)))


def _inproj_kernel(x_ref, w_ref, *refs, widths, entry_norm):
    if entry_norm:
        g_ref, b_ref, qa_ref, qb_ref, cv_ref, sg_ref, f_ref, xn_ref = refs
        x = _layer_norm_rows(x_ref[...], g_ref[...], b_ref[...])
        xn_ref[...] = x
    else:
        qa_ref, qb_ref, cv_ref, sg_ref, f_ref = refs
        x = x_ref[...]
    xb = x.astype(BF16)
    off = 0
    for ref, wd in zip((qa_ref, qb_ref, cv_ref, sg_ref, f_ref), widths):
        ref[...] = jnp.dot(xb, w_ref[:, off:off + wd], preferred_element_type=F32).astype(ref.dtype)
        off += wd


def _inproj_call(x, w_all, W, entry_ln=None):
    T, D = x.shape
    widths = (3 * W, 3 * W, 3 * W, 2 * W, LANES)
    NC = sum(widths)
    dts = (BF16, BF16, BF16, BF16, F32)
    row_spec = pl.BlockSpec((ROW_TILE, D), lambda i: (i, 0))
    vec_spec = pl.BlockSpec((1, D), lambda i: (0, 0))
    in_specs = [row_spec, pl.BlockSpec((D, NC), lambda i: (0, 0))]
    out_shape = [jax.ShapeDtypeStruct((T, wd), dt) for wd, dt in zip(widths, dts)]
    out_specs = [pl.BlockSpec((ROW_TILE, wd), lambda i: (i, 0)) for wd in widths]
    args = [x, w_all]
    if entry_ln is not None:
        in_specs += [vec_spec, vec_spec]
        out_shape.append(jax.ShapeDtypeStruct((T, D), F32))
        out_specs.append(row_spec)
        args += [entry_ln[0].reshape(1, D), entry_ln[1].reshape(1, D)]
    return pl.pallas_call(
        functools.partial(_inproj_kernel, widths=widths, entry_norm=entry_ln is not None),
        out_shape=out_shape,
        grid=(T // ROW_TILE,),
        in_specs=in_specs,
        out_specs=out_specs,
        compiler_params=_cparams(("parallel",), VMEM_LIMIT),
        name="inproj",
    )(*args)


def _fcum_kernel(f_ref, b_ref, tri_ref, c_ref):
    R, S = f_ref.shape
    y = f_ref[...] + b_ref[...]
    lf = -_softplus(-y)
    tri = tri_ref[...]
    carry = jnp.zeros((R, 1), F32)
    for blk in range(S // LANES):
        seg = lf[:, blk * LANES:(blk + 1) * LANES]
        s1 = seg.astype(BF16)
        r1 = seg - s1.astype(F32)
        s2 = r1.astype(BF16)
        s3 = (r1 - s2.astype(F32)).astype(BF16)
        cs = (jnp.dot(s1, tri, preferred_element_type=F32)
              + jnp.dot(s2, tri, preferred_element_type=F32)
              + jnp.dot(s3, tri, preferred_element_type=F32)) + carry
        c_ref[:, blk * LANES:(blk + 1) * LANES] = cs
        carry = cs[:, LANES - 1:LANES]


def _fcum_call(f_rows, b_rows):
    R, S = f_rows.shape
    idx = jnp.arange(LANES)
    tri = (idx[:, None] <= idx[None, :]).astype(BF16)
    return pl.pallas_call(
        _fcum_kernel,
        out_shape=jax.ShapeDtypeStruct((R, S), F32),
        name="forget_cumsum",
    )(f_rows, b_rows, tri)


def _rms_group_store(o_ref, out, g_ref):
    ms = jnp.mean(out * out, axis=-1, keepdims=True)
    o_ref[...] = (out * lax.rsqrt(ms + RMS_EPS) * g_ref[...]).astype(o_ref.dtype)


def _head_masks():
    lane = lax.broadcasted_iota(jnp.int32, (1, LANES), 1)
    return [lane < HEAD_DIM, lane >= HEAD_DIM]


def _nt_dot(a, b):
    return lax.dot_general(a, b, (((1,), (1,)), ((), ())), preferred_element_type=F32)


SB_DEAD = -104.0


def _neg_softplus(z):
    return jnp.minimum(-z, 0.0) - jnp.log(1.0 + jnp.exp(-jnp.abs(z)))


def _sb_attn_kernel(q_ref, k_ref, v_ref, m_ref, g_ref, o_ref):
    Q = q_ref.shape[0]
    assert Q == ATT_K
    i = pl.program_id(1)
    row = lax.broadcasted_iota(jnp.int32, (Q, ATT_K), 0)
    col = lax.broadcasted_iota(jnp.int32, (Q, ATT_K), 1)
    diag_mask = col < row
    hmasks = _head_masks()

    def tile(qms, j, state, mask):
        ks = pl.multiple_of(j * ATT_K, ATT_K)
        k_ts = [k_ref[pl.ds(ks, ATT_K), p * LANES:(p + 1) * LANES] for p in range(HEADS // 2)]
        v_ts = [v_ref[pl.ds(ks, ATT_K), p * LANES:(p + 1) * LANES] for p in range(HEADS // 2)]
        later_mat = m_ref[...]
        zs = [_nt_dot(qms[h], k_ts[h // 2]) for h in range(HEADS)]
        mid = []
        for h in range(HEADS):
            log_1m = _neg_softplus(zs[h])
            log_b = zs[h] + log_1m
            if mask is not None:
                log_1m = jnp.where(mask, log_1m, 0.0)
            hi = log_1m.astype(BF16)
            lo = (log_1m - hi.astype(F32)).astype(BF16)
            later = (jnp.dot(hi, later_mat, preferred_element_type=F32)
                     + jnp.dot(lo, later_mat, preferred_element_type=F32)) + state[2 * h]
            mid.append((log_b, later, log_1m[:, 0:1]))
        out = []
        for h in range(HEADS):
            log_b, later, first = mid[h]
            w = jnp.exp(log_b + later)
            if mask is not None:
                w = jnp.where(mask, w, 0.0)
            acc = state[2 * h + 1] + jnp.dot(w.astype(BF16), v_ts[h // 2], preferred_element_type=F32)
            out.extend((later[:, 0:1] + first, acc))
        return tuple(out)

    qms = []
    for h in range(HEADS):
        q_pair = q_ref[:, (h // 2) * LANES:(h // 2 + 1) * LANES]
        qms.append(jnp.where(hmasks[h % 2], q_pair, jnp.zeros_like(q_pair)))
    def alive(state):
        top = state[0]
        for h in range(1, HEADS):
            top = jnp.maximum(top, state[2 * h])
        return (jnp.max(top) >= SB_DEAD).astype(jnp.int32)

    state = (jnp.zeros((Q, 1), F32), jnp.zeros((Q, LANES), F32)) * HEADS
    state = tile(qms, i, state, diag_mask)

    def cond(c):
        return (c[0] <= i) & (c[1] > 0)

    def body(c):
        st = tile(qms, i - c[0], tuple(c[2:]), None)
        return (c[0] + 1, alive(st)) + st

    state = lax.while_loop(cond, body, (jnp.int32(1), alive(state)) + state)[2:]
    outs = [jnp.where(hmasks[0], state[4 * p + 1], state[4 * p + 3]) for p in range(HEADS // 2)]
    _rms_group_store(o_ref, jnp.concatenate(outs, axis=-1), g_ref)


def _fox_attn_kernel(q_ref, k_ref, v_ref, c_ref, g_ref, o_ref):
    Q = q_ref.shape[0]
    assert Q == ATT_K
    i = pl.program_id(1)
    nk = c_ref.shape[1] // HEADS
    row = lax.broadcasted_iota(jnp.int32, (Q, ATT_K), 0)
    col = lax.broadcasted_iota(jnp.int32, (Q, ATT_K), 1)
    diag_mask = col <= row
    hmasks = _head_masks()
    neg = jnp.finfo(F32).min

    def tile(qms, j, state, mask):
        ks = pl.multiple_of(j * ATT_K, ATT_K)
        k_ts = [k_ref[pl.ds(ks, ATT_K), p * LANES:(p + 1) * LANES] for p in range(HEADS // 2)]
        v_ts = [v_ref[pl.ds(ks, ATT_K), p * LANES:(p + 1) * LANES] for p in range(HEADS // 2)]
        ss = []
        for h in range(HEADS):
            c_row = c_ref[0, pl.ds(h * nk + j, 1), :]
            s = _nt_dot(qms[h], k_ts[h // 2]) - c_row
            if mask is not None:
                s = jnp.where(mask, s, neg)
            ss.append(s)
        mid = []
        for h in range(HEADS):
            m, l = state[3 * h], state[3 * h + 1]
            m_new = jnp.maximum(m, jnp.max(ss[h], axis=-1, keepdims=True))
            alpha = jnp.exp(m - m_new)
            p = jnp.exp(ss[h] - m_new)
            l = alpha * l + jnp.sum(p, axis=-1, keepdims=True)
            mid.append((m_new, l, alpha, p.astype(BF16)))
        out = []
        for h in range(HEADS):
            m_new, l, alpha, p = mid[h]
            acc = alpha * state[3 * h + 2] + jnp.dot(p, v_ts[h // 2], preferred_element_type=F32)
            out.extend((m_new, l, acc))
        return tuple(out)

    qms = []
    for h in range(HEADS):
        q_pair = q_ref[:, (h // 2) * LANES:(h // 2 + 1) * LANES]
        qms.append(jnp.where(hmasks[h % 2], q_pair, jnp.zeros_like(q_pair)))
    state = (jnp.full((Q, 1), neg, F32), jnp.zeros((Q, 1), F32), jnp.zeros((Q, LANES), F32)) * HEADS
    state = tile(qms, i, state, diag_mask)
    state = lax.fori_loop(1, i + 1, lambda jj, st: tile(qms, i - jj, st, None), state)
    heads = [state[3 * h + 2] / state[3 * h + 1] for h in range(HEADS)]
    outs = [jnp.where(hmasks[0], heads[2 * p], heads[2 * p + 1]) for p in range(HEADS // 2)]
    _rms_group_store(o_ref, jnp.concatenate(outs, axis=-1), g_ref)


def _attn_specs(B, S, W):
    nq = S // ATT_Q
    q_spec = pl.BlockSpec((ATT_Q, W), lambda b, i: (b * nq + i, 0))
    k_spec = pl.BlockSpec((S, W), lambda b, i: (b, 1))
    v_spec = pl.BlockSpec((S, W), lambda b, i: (b, 2))
    g_spec = pl.BlockSpec((1, W), lambda b, i: (0, 0))
    o_spec = pl.BlockSpec((ATT_Q, W), lambda b, i: (b * nq + i, 0))
    return nq, q_spec, k_spec, v_spec, g_spec, o_spec


def _sb_attn_call(qkv, gain, B, S, W):
    T = B * S
    nq, q_spec, k_spec, v_spec, g_spec, o_spec = _attn_specs(B, S, W)
    idx = jnp.arange(ATT_K)
    later_mat = (idx[:, None] > idx[None, :]).astype(BF16)
    return pl.pallas_call(
        _sb_attn_kernel,
        out_shape=jax.ShapeDtypeStruct((T, W), BF16),
        grid=(B, nq),
        in_specs=[q_spec, k_spec, v_spec,
                  pl.BlockSpec((ATT_K, ATT_K), lambda b, i: (0, 0)), g_spec],
        out_specs=o_spec,
        compiler_params=_cparams(("parallel", "arbitrary"), VMEM_LIMIT),
        name="sb_attn",
    )(qkv, qkv, qkv, later_mat, gain)


def _fox_attn_call(qkv, c_tiles, gain, B, S, W):
    T = B * S
    nq, q_spec, k_spec, v_spec, g_spec, o_spec = _attn_specs(B, S, W)
    rows = c_tiles.shape[1]
    return pl.pallas_call(
        _fox_attn_kernel,
        out_shape=jax.ShapeDtypeStruct((T, W), BF16),
        grid=(B, nq),
        in_specs=[q_spec, k_spec, v_spec,
                  pl.BlockSpec((1, rows, ATT_K), lambda b, i: (b, 0, 0)), g_spec],
        out_specs=o_spec,
        compiler_params=_cparams(("parallel", "arbitrary"), VMEM_LIMIT),
        name="fox_attn",
    )(qkv, qkv, qkv, c_tiles, gain)


HALO = 16


def _gelu_tanh(x):
    c = math.sqrt(2.0 / math.pi)
    return 0.5 * x * (1.0 + jnp.tanh(c * (x + 0.044715 * (x * x * x))))


def _conv_sgu_kernel(cv_ref, halo_ref, sg_ref, cw_ref, cb_ref, lg_ref, lb_ref, ws_ref, bs_ref,
                     gc_ref, gd_ref, oc_ref, od_ref, *, tiles_per_seq):
    R, W3 = cv_ref.shape
    W = W3 // 3
    i = pl.program_id(0)
    z = cv_ref[:, 2 * W:3 * W].astype(F32) * cv_ref[:, 0:W].astype(F32)
    zh = halo_ref[:, 2 * W:3 * W].astype(F32) * halo_ref[:, 0:W].astype(F32)
    zh = jnp.where(i % tiles_per_seq == 0, jnp.zeros_like(zh), zh)
    zf = jnp.concatenate([zh, z], axis=0)
    z1 = pltpu.roll(zf, 1, 0)[HALO:]
    z2 = pltpu.roll(zf, 2, 0)[HALO:]
    y = cb_ref[...] + cw_ref[0:1, :] * z2
    y = y + cw_ref[1:2, :] * z1
    y = y + cw_ref[2:3, :] * z
    _rms_group_store(oc_ref, cv_ref[:, W:2 * W].astype(F32) * y, gc_ref)
    gel = _gelu_tanh(sg_ref[...].astype(F32))
    u = gel[:, 0:W]
    vn = _layer_norm_rows(gel[:, W:2 * W], lg_ref[...], lb_ref[...]).astype(BF16)
    lane = lax.broadcasted_iota(jnp.int32, (1, W), 1)
    pi = lax.broadcasted_iota(jnp.int32, (SGU_BLOCK, SGU_BLOCK), 0) // CHUNK
    pj = lax.broadcasted_iota(jnp.int32, (SGU_BLOCK, SGU_BLOCK), 1) // CHUNK
    w_m = [jnp.where(pj <= pi, ws_ref[g], 0.0).astype(BF16) for g in range(HEADS)]
    blocks = []
    for n in range(R // SGU_BLOCK):
        vb = vn[n * SGU_BLOCK:(n + 1) * SGU_BLOCK, :]
        mixed = jnp.dot(w_m[0], vb, preferred_element_type=F32)
        for g in range(1, HEADS):
            mg = jnp.dot(w_m[g], vb, preferred_element_type=F32)
            mixed = jnp.where(lane >= g * HEAD_DIM, mg, mixed)
        blocks.append(mixed + bs_ref[...])
    mixed = jnp.concatenate(blocks, axis=0)
    _rms_group_store(od_ref, u * mixed, gd_ref)


def _conv_sgu_call(cv, sg, conv_w, conv_b, ln_g, ln_b, sgu_w, sgu_b, g_c, g_d, S, W):
    T = cv.shape[0]
    R = ROW_TILE
    tiles_per_seq = S // R
    bias_tile = jnp.repeat(sgu_b.T, HEAD_DIM, axis=1)
    row = lambda a: a.reshape(1, W)
    const = lambda shape: pl.BlockSpec(shape, lambda i: tuple(0 for _ in shape))
    return pl.pallas_call(
        functools.partial(_conv_sgu_kernel, tiles_per_seq=tiles_per_seq),
        out_shape=[jax.ShapeDtypeStruct((T, W), BF16)] * 2,
        grid=(T // R,),
        in_specs=[pl.BlockSpec((R, 3 * W), lambda i: (i, 0)),
                  pl.BlockSpec((HALO, 3 * W), lambda i: (jnp.maximum(i * (R // HALO) - 1, 0), 0)),
                  pl.BlockSpec((R, 2 * W), lambda i: (i, 0)),
                  const((CONV_WIDTH, W)), const((1, W)), const((1, W)), const((1, W)),
                  const((HEADS, SGU_BLOCK, SGU_BLOCK)), const((SGU_BLOCK, W)),
                  const((1, W)), const((1, W))],
        out_specs=[pl.BlockSpec((R, W), lambda i: (i, 0))] * 2,
        compiler_params=_cparams(("parallel",), VMEM_LIMIT),
        name="conv_sgu",
    )(cv, cv, sg, conv_w, row(conv_b), row(ln_g), row(ln_b), sgu_w, bias_tile, row(g_c), row(g_d))


ROUTE_CHUNK = ROW_TILE
PAIRS_PER_GROUP = EXPERTS_PER_GROUP * (EXPERTS_PER_GROUP - 1) // 2
N_PAIRS = N_EXPERT_GROUPS * PAIRS_PER_GROUP


def _outproj_router_kernel(ma_ref, mb_ref, mc_ref, md_ref, wo_ref, x_ref, g_ref, b_ref,
                           rwa_ref, rwh_ref, rb_ref, tri_ref, x1_ref, route_ref, cnt_ref, base_ref, *, alpha):
    @pl.when(pl.program_id(0) == 0)
    def _():
        base_ref[...] = jnp.zeros(base_ref.shape, base_ref.dtype)

    base = base_ref[...]
    for c in range(x_ref.shape[0] // ROUTE_CHUNK):
        rows = slice(c * ROUTE_CHUNK, (c + 1) * ROUTE_CHUNK)
        base = _outproj_route_rows(rows, ma_ref, mb_ref, mc_ref, md_ref, wo_ref, x_ref, g_ref, b_ref,
                                   rwa_ref, rwh_ref, rb_ref, tri_ref, x1_ref, route_ref, base, alpha)
    base_ref[...] = base
    cnt_ref[...] = jnp.broadcast_to(base, cnt_ref.shape)


def _outproj_route_rows(rows, ma_ref, mb_ref, mc_ref, md_ref, wo_ref, x_ref, g_ref, b_ref,
                        rwa_ref, rwh_ref, rb_ref, tri_ref, x1_ref, route_ref, base, alpha):
    mix = jnp.concatenate([ref[rows, :] for ref in (ma_ref, mb_ref, mc_ref, md_ref)], axis=1)
    acc = jnp.dot(mix, wo_ref[...], preferred_element_type=F32)
    x1 = _layer_norm_rows(alpha * x_ref[rows, :] + acc, g_ref[...], b_ref[...])
    D = x1.shape[1]
    x1_ref[rows, 0:D] = x1
    xh = x1.astype(BF16)
    xl = (x1 - xh.astype(F32)).astype(BF16)
    both = jnp.dot(xh, rwa_ref[...], preferred_element_type=F32)
    logits = (both[:, 0:ROUTE_LANES] + both[:, ROUTE_LANES:2 * ROUTE_LANES]
              + jnp.dot(xl, rwh_ref[...], preferred_element_type=F32)) + rb_ref[...]
    R = logits.shape[0]
    lane = lax.broadcasted_iota(jnp.int32, (R, ROUTE_LANES), 1).astype(F32)
    big = float(ROUTE_LANES)
    ninf = jnp.finfo(F32).min
    gmask = lane < N_EXPERT_GROUPS
    lg = jnp.where(gmask, logits, ninf)
    mg = jnp.max(lg, axis=-1, keepdims=True)
    sum_g = jnp.sum(jnp.where(gmask, jnp.exp(lg - mg), 0.0), axis=-1, keepdims=True)
    p_gsel = 1.0 / sum_g
    g_sel = jnp.min(jnp.where(lg == mg, lane, big), axis=-1, keepdims=True)
    e_lo = N_EXPERT_GROUPS + EXPERTS_PER_GROUP * g_sel
    emask = (lane >= e_lo) & (lane < e_lo + EXPERTS_PER_GROUP)
    le = jnp.where(emask, logits, ninf)
    me = jnp.max(le, axis=-1, keepdims=True)
    ee = jnp.where(emask, jnp.exp(le - me), 0.0)
    pe = ee / jnp.sum(ee, axis=-1, keepdims=True)
    pe = jnp.where(emask, pe, -1.0)
    p1 = jnp.max(pe, axis=-1, keepdims=True)
    i1 = jnp.min(jnp.where(pe == p1, lane, big), axis=-1, keepdims=True)
    pe2 = jnp.where(lane == i1, -1.0, pe)
    p2 = jnp.max(pe2, axis=-1, keepdims=True)
    i2 = jnp.min(jnp.where(pe2 == p2, lane, big), axis=-1, keepdims=True)
    denom = p1 + p2
    gate1 = p_gsel * p1 / denom
    gate2 = p_gsel * p2 / denom
    a1 = i1 - e_lo
    a2 = i2 - e_lo
    lo = jnp.minimum(a1, a2)
    hi = jnp.maximum(a1, a2)
    first_low = a1 < a2
    gate_lo = jnp.where(first_low, gate1, gate2)
    gate_hi = jnp.where(first_low, gate2, gate1)
    pair = g_sel * PAIRS_PER_GROUP + lo * ((2 * EXPERTS_PER_GROUP - 1) - lo) * 0.5 + (hi - lo - 1.0)
    ex_lo = g_sel * EXPERTS_PER_GROUP + lo
    ex_hi = g_sel * EXPERTS_PER_GROUP + hi
    sel = lane == pair
    one = jnp.where(sel, 1.0, 0.0)
    before = jnp.dot(tri_ref[...], one.astype(BF16), preferred_element_type=F32) + base
    rank = jnp.sum(jnp.where(sel, before, 0.0), axis=-1, keepdims=True)
    route = jnp.where(lane == 0, ex_lo, jnp.where(lane == 1, ex_hi, jnp.where(lane == 2, gate_lo,
            jnp.where(lane == 3, gate_hi, jnp.where(lane == 4, rank, pair)))))
    route_ref[rows, :] = route
    x1_ref[rows, D:D + ROUTE_LANES] = route
    return before[R - 1:R, :] + one[R - 1:R, :]


def _outproj_router_call(mixes, w_out, x, g, b, rw, rb, alpha):
    T, D = x.shape
    W = mixes[0].shape[1]
    R = ROUTE_CHUNK
    rw_hi = rw.astype(BF16)
    rw_lo = (rw - rw_hi.astype(F32)).astype(BF16)
    rwa = jnp.concatenate([rw_hi, rw_lo], axis=1)
    idx = jnp.arange(ROUTE_CHUNK)
    tri = (idx[None, :] < idx[:, None]).astype(BF16)
    const = lambda shape: pl.BlockSpec(shape, lambda i: tuple(0 for _ in shape))
    return pl.pallas_call(
        functools.partial(_outproj_router_kernel, alpha=alpha),
        out_shape=[jax.ShapeDtypeStruct((T, D + ROUTE_LANES), F32), jax.ShapeDtypeStruct((T, ROUTE_LANES), F32),
                   jax.ShapeDtypeStruct((8, ROUTE_LANES), F32)],
        grid=(T // R,),
        in_specs=[pl.BlockSpec((R, W), lambda i: (i, 0))] * 4
        + [const((D, D)), pl.BlockSpec((R, D), lambda i: (i, 0)), const((1, D)), const((1, D)),
           const((D, 2 * ROUTE_LANES)), const((D, ROUTE_LANES)), const((1, ROUTE_LANES)),
           const((ROUTE_CHUNK, ROUTE_CHUNK))],
        out_specs=[pl.BlockSpec((R, D + ROUTE_LANES), lambda i: (i, 0)),
                   pl.BlockSpec((R, ROUTE_LANES), lambda i: (i, 0)),
                   const((8, ROUTE_LANES))],
        scratch_shapes=[pltpu.VMEM((1, ROUTE_LANES), F32)],
        compiler_params=_cparams(("arbitrary",), VMEM_LIMIT),
        name="outproj_router",
    )(*mixes, w_out, x, g.reshape(1, D), b.reshape(1, D), rwa, rw_hi, rb, tri)


IDX_SLOTS = 3
SUBLANES = 8


def _dispatch_kernel(fill_ref, dest_hbm, x_ref, xd_hbm, dsm, zbuf, sem_idx, sem_row, sem_fill):
    i = pl.program_id(0)
    n = pl.num_programs(0)
    R = x_ref.shape[0] * SUBLANES
    TB = zbuf.shape[0]
    NI = R

    def idx_copy(t):
        s = t % 2
        return pltpu.make_async_copy(dest_hbm.at[t], dsm.at[pl.ds(s * NI, NI)], sem_idx.at[s])

    def fill_copy(blk):
        return pltpu.make_async_copy(zbuf, xd_hbm.at[pl.ds(blk * TB, TB)], sem_fill)

    @pl.when(i == 0)
    def _():
        idx_copy(0).start()
        zbuf[...] = jnp.zeros(zbuf.shape, zbuf.dtype)

        def start_body(blk, _):
            @pl.when(fill_ref[blk] != 0)
            def _():
                fill_copy(blk).start()
            return 0

        def wait_body(blk, _):
            @pl.when(fill_ref[blk] != 0)
            def _():
                fill_copy(blk).wait()
            return 0

        lax.fori_loop(0, fill_ref.shape[0], start_body, 0)
        lax.fori_loop(0, fill_ref.shape[0], wait_body, 0)

    idx_copy(i).wait()

    @pl.when(i + 1 < n)
    def _():
        idx_copy(i + 1).start()

    base = (i % 2) * NI

    def body(grp, _):
        at = base + SUBLANES * grp
        for u in range(SUBLANES):
            d = dsm[at + u]
            pltpu.make_async_copy(
                x_ref.at[grp, pl.ds(u, 1), :],
                xd_hbm.at[lax.shift_right_logical(d, 3), pl.ds(d & (SUBLANES - 1), 1), :],
                sem_row).start(priority=u % 2)
        return 0

    lax.fori_loop(0, R // SUBLANES, body, 0)
    pltpu.make_async_copy(x_ref, x_ref, sem_row).wait()


def _dispatch_call(fill_flag, dest, x1, n_rows):
    T, D = x1.shape
    R = DISPATCH_TILE
    dest = dest.reshape(T // R, R)
    grid_spec = pltpu.PrefetchScalarGridSpec(
        num_scalar_prefetch=1,
        grid=(T // R,),
        in_specs=[pl.BlockSpec(memory_space=pl.ANY),
                  pl.BlockSpec((R // SUBLANES, SUBLANES, D), lambda i, ff: (i, 0, 0))],
        out_specs=pl.BlockSpec(memory_space=pl.ANY),
        scratch_shapes=[pltpu.SMEM((2 * R,), jnp.int32),
                        pltpu.VMEM((EXPERT_ROWS // SUBLANES, SUBLANES, D), F32),
                        pltpu.SemaphoreType.DMA((2,)),
                        pltpu.SemaphoreType.DMA,
                        pltpu.SemaphoreType.DMA],
    )
    xd = pl.pallas_call(
        _dispatch_kernel,
        out_shape=jax.ShapeDtypeStruct((n_rows // SUBLANES, SUBLANES, D), F32),
        grid_spec=grid_spec,
        compiler_params=_cparams(("arbitrary",), VMEM_LIMIT),
        name="moe_dispatch",
    )(fill_flag, dest, x1.reshape(T // SUBLANES, SUBLANES, D))
    return xd.reshape(n_rows, D)


def _experts_kernel(ea_ref, eb_ref, act_ref, x_ref, w1_ref, w3_ref, w2_ref, y_ref, wb1, wb3, wb2):
    i = pl.program_id(0)
    D = y_ref.shape[1]
    RB = EXPERT_ROWS
    per_step = x_ref.shape[0] // RB
    first = i * per_step
    group = ea_ref[first] // EXPERTS_PER_GROUP
    prev = ea_ref[jnp.maximum(first, 1) - 1] // EXPERTS_PER_GROUP

    @pl.when((i == 0) | (group != prev))
    def _():
        for e in range(EXPERTS_PER_GROUP):
            wb1[e] = w1_ref[0, e].astype(BF16)
            wb3[e] = w3_ref[0, e].astype(BF16)
            wb2[e] = w2_ref[0, e].astype(BF16)

    def expert(xb, e):
        h1 = jnp.dot(xb, wb1[e], preferred_element_type=F32)
        h3 = jnp.dot(xb, wb3[e], preferred_element_type=F32)
        h = (h1 * jax.nn.sigmoid(h1)) * h3
        return jnp.dot(h.astype(BF16), wb2[e], preferred_element_type=F32)

    for s in range(per_step):
        blk = first + s
        rows = slice(s * RB, (s + 1) * RB)

        @pl.when(act_ref[blk] != 0)
        def _(blk=blk, rows=rows):
            xb = x_ref[rows, 0:D].astype(BF16)
            ya = expert(xb, ea_ref[blk] % EXPERTS_PER_GROUP) * x_ref[rows, D + 2:D + 3]
            yb = expert(xb, eb_ref[blk] % EXPERTS_PER_GROUP) * x_ref[rows, D + 3:D + 4]
            y_ref[rows, :] = ya + yb

        @pl.when(act_ref[blk] == 0)
        def _(rows=rows):
            y_ref[rows, :] = jnp.zeros((RB, D), y_ref.dtype)


def _experts_call(block_a, block_b, block_active, xd, w1, w3, w2, layer):
    P, DX = xd.shape
    D = DX - ROUTE_LANES
    SR = EXPERT_STEP_ROWS
    per_step = SR // EXPERT_ROWS
    DE = w1.shape[3]
    EG = EXPERTS_PER_GROUP
    by_group = lambda w: w.reshape(w.shape[0] * N_EXPERT_GROUPS, EG, w.shape[2], w.shape[3])
    wmap = lambda i, ea, eb, act: (layer * N_EXPERT_GROUPS + ea[i * per_step] // EG, 0, 0, 0)
    once = pl.Buffered(1)
    grid_spec = pltpu.PrefetchScalarGridSpec(
        num_scalar_prefetch=3,
        grid=(P // SR,),
        in_specs=[pl.BlockSpec((SR, DX), lambda i, ea, eb, act: (i, 0)),
                  pl.BlockSpec((1, EG, D, DE), wmap, pipeline_mode=once),
                  pl.BlockSpec((1, EG, D, DE), wmap, pipeline_mode=once),
                  pl.BlockSpec((1, EG, DE, D), wmap, pipeline_mode=once)],
        out_specs=pl.BlockSpec((SR, D), lambda i, ea, eb, act: (i, 0)),
        scratch_shapes=[pltpu.VMEM((EG, D, DE), BF16), pltpu.VMEM((EG, D, DE), BF16),
                        pltpu.VMEM((EG, DE, D), BF16)],
    )
    return pl.pallas_call(
        _experts_kernel,
        out_shape=jax.ShapeDtypeStruct((P, D), F32),
        grid_spec=grid_spec,
        compiler_params=_cparams(("arbitrary",), EXPERT_VMEM_LIMIT),
        name="experts",
    )(block_a, block_b, block_active, xd, by_group(w1), by_group(w3), by_group(w2))


def _combine_kernel(dest_hbm, y_hbm, x_ref, g_ref, b_ref, o_ref, dsm, ybuf, sem_idx, sem_row, *, alpha):
    i = pl.program_id(0)
    n = pl.num_programs(0)
    G = x_ref.shape[0]
    D = o_ref.shape[2]
    NI = G * SUBLANES

    def idx_copy(t):
        s = t % IDX_SLOTS
        return pltpu.make_async_copy(dest_hbm.at[t], dsm.at[pl.ds(s * NI, NI)], sem_idx.at[s])

    def issue_gather(t):
        base = (t % IDX_SLOTS) * NI
        s2 = t % 2

        def body(grp, _):
            at = base + SUBLANES * grp
            for u in range(SUBLANES):
                d = dsm[at + u]
                pltpu.make_async_copy(
                    y_hbm.at[lax.shift_right_logical(d, 3), pl.ds(d & (SUBLANES - 1), 1), :],
                    ybuf.at[s2, grp, pl.ds(u, 1), :], sem_row.at[s2]).start(priority=u % 2)
            return 0

        lax.fori_loop(0, G, body, 0)

    @pl.when(i == 0)
    def _():
        c0 = idx_copy(0)
        c0.start()
        c0.wait()
        issue_gather(0)

        @pl.when(n > 1)
        def _():
            idx_copy(1).start()

    @pl.when(i + 1 < n)
    def _():
        idx_copy(i + 1).wait()
        issue_gather(i + 1)

    @pl.when(i + 2 < n)
    def _():
        idx_copy(i + 2).start()

    s2 = i % 2
    pltpu.make_async_copy(ybuf.at[s2], ybuf.at[s2], sem_row.at[s2]).wait()
    o_ref[...] = _layer_norm_rows(alpha * x_ref[:, :, 0:D] + ybuf[s2], g_ref[...], b_ref[...])


def _combine_call(dest, y, x1e, g, b, alpha):
    T = x1e.shape[0]
    D = y.shape[1]
    R = ROW_TILE
    dest = dest.reshape(T // R, R)
    G = R // SUBLANES
    tiles = lambda a: a.reshape(a.shape[0] // SUBLANES, SUBLANES, a.shape[1])
    rows_spec = lambda lanes: pl.BlockSpec((G, SUBLANES, lanes), lambda i: (i, 0, 0))
    vec_spec = pl.BlockSpec((1, 1, D), lambda i: (0, 0, 0))
    out = pl.pallas_call(
        functools.partial(_combine_kernel, alpha=alpha),
        out_shape=jax.ShapeDtypeStruct((T // SUBLANES, SUBLANES, D), F32),
        grid=(T // R,),
        in_specs=[pl.BlockSpec(memory_space=pl.ANY), pl.BlockSpec(memory_space=pl.ANY),
                  rows_spec(x1e.shape[1]), vec_spec, vec_spec],
        out_specs=rows_spec(D),
        scratch_shapes=[pltpu.SMEM((IDX_SLOTS * R,), jnp.int32),
                        pltpu.VMEM((2, G, SUBLANES, D), F32),
                        pltpu.SemaphoreType.DMA((IDX_SLOTS,)),
                        pltpu.SemaphoreType.DMA((2,))],
        compiler_params=_cparams(("arbitrary",), VMEM_LIMIT),
        name="combine_ln",
    )(dest, tiles(y), tiles(x1e), g.reshape(1, 1, D), b.reshape(1, 1, D))
    return out.reshape(T, D)


def _pair_expert_tables():
    lo_hi = [(lo, hi) for lo in range(EXPERTS_PER_GROUP) for hi in range(lo + 1, EXPERTS_PER_GROUP)]
    ea = [g * EXPERTS_PER_GROUP + lo for g in range(N_EXPERT_GROUPS) for lo, _ in lo_hi]
    eb = [g * EXPERTS_PER_GROUP + hi for g in range(N_EXPERT_GROUPS) for _, hi in lo_hi]
    return jnp.asarray(ea, jnp.int32), jnp.asarray(eb, jnp.int32)


def _dispatch_tables(route, cnt, T):
    RB = EXPERT_ROWS
    SR = EXPERT_STEP_ROWS
    NG, PG = N_EXPERT_GROUPS, PAIRS_PER_GROUP
    pairs = jnp.arange(N_PAIRS, dtype=jnp.int32)
    counts = cnt[0, 0:N_PAIRS].astype(jnp.int32)
    padded = ((counts + RB - 1) // RB * RB).reshape(NG, PG)
    in_group_end = jnp.cumsum(padded, axis=1)
    group_rows = (in_group_end[:, -1] + SR - 1) // SR * SR
    group_end = jnp.cumsum(group_rows)
    group_start = group_end - group_rows
    p_starts = (group_start[:, None] + in_group_end - padded).reshape(N_PAIRS)
    p_ends = p_starts + padded.reshape(N_PAIRS)
    rank = route[:, 4].astype(jnp.int32)
    pair = route[:, 5].astype(jnp.int32)
    start_of = jnp.sum(jnp.where(pair[:, None] == pairs[None, :], p_starts[None, :], 0), axis=-1)
    dest = start_of + rank
    P = -(-(T + N_PAIRS * RB + NG * SR) // SR) * SR
    n_blocks = P // RB
    row0 = jnp.arange(n_blocks, dtype=jnp.int32) * RB
    inside = (row0[:, None] >= p_starts[None, :]) & (row0[:, None] < p_ends[None, :])
    ea, eb = _pair_expert_tables()
    block_group = jnp.minimum(jnp.sum((group_end[None, :] <= row0[:, None]).astype(jnp.int32), axis=1), NG - 1)
    block_active = jnp.any(inside, axis=1)
    block_a = jnp.where(block_active, jnp.sum(jnp.where(inside, ea[None, :], 0), axis=1),
                        block_group * EXPERTS_PER_GROUP)
    block_b = jnp.where(block_active, jnp.sum(jnp.where(inside, eb[None, :], 0), axis=1),
                        block_group * EXPERTS_PER_GROUP)
    last_partial = jnp.any((row0[:, None] == (p_ends - RB)[None, :]) & ((counts % RB) != 0)[None, :], axis=1)
    fill_flag = (last_partial | ~block_active).astype(jnp.int32)
    return dest, block_a, block_b, block_active.astype(jnp.int32), fill_flag, P


def kernel(x, ln_in_g, ln_in_b, w_in, b_f, conv_w, conv_b, sgu_ln_g, sgu_ln_b, sgu_w, sgu_b, grp_g, w_out, ln1_g, ln1_b, router_g_w, router_g_b, router_e_w, router_e_b, w1, w3, w2, ln2_g, ln2_b):
    B, S, D = x.shape
    depth = w_in.shape[0]
    W = D // N_MIXERS
    T = B * S
    alpha = (2.0 * depth) ** 0.25
    scale = 1.0 / math.sqrt(HEAD_DIM)

    xs = x.reshape(T, D)
    for l in range(depth):
        wl = w_in[l]
        o_f = 6 * W
        o_cv = o_f + HEADS
        o_sg = o_cv + 3 * W
        f_cols = jnp.pad(wl[:, o_f:o_cv], ((0, 0), (0, LANES - HEADS)))
        w_all = jnp.concatenate(
            [wl[:, 0:W] * scale, wl[:, W:3 * W], wl[:, 3 * W:4 * W] * scale, wl[:, 4 * W:6 * W],
             wl[:, o_cv:o_sg], wl[:, o_sg:], f_cols], axis=1).astype(BF16)
        rw = jnp.pad(jnp.concatenate([router_g_w[l], router_e_w[l]], axis=1),
                     ((0, 0), (0, ROUTE_LANES - N_EXPERT_GROUPS - N_EXPERTS)))
        rb = jnp.pad(jnp.concatenate([router_g_b[l], router_e_b[l]]),
                     (0, ROUTE_LANES - N_EXPERT_GROUPS - N_EXPERTS)).reshape(1, ROUTE_LANES)
        gains = [grp_g[l, k * W:(k + 1) * W].reshape(1, W) for k in range(N_MIXERS)]

        if l == 0:
            qkv_a, qkv_b, cv, sg, f_pad, xs = _inproj_call(xs, w_all, W, entry_ln=(ln_in_g, ln_in_b))
        else:
            qkv_a, qkv_b, cv, sg, f_pad = _inproj_call(xs, w_all, W)
        f_rows = f_pad[:, 0:HEADS].reshape(B, S, HEADS).transpose(0, 2, 1).reshape(B * HEADS, S)
        b_rows = jnp.tile(b_f[l], B).reshape(B * HEADS, 1)
        c = _fcum_call(f_rows, b_rows)
        c_tiles = c.reshape(B, HEADS * (S // ATT_K), ATT_K)
        mix_a = _sb_attn_call(qkv_a, gains[0], B, S, W)
        mix_b = _fox_attn_call(qkv_b, c_tiles, gains[1], B, S, W)
        mix_c, mix_d = _conv_sgu_call(cv, sg, conv_w[l], conv_b[l], sgu_ln_g[l], sgu_ln_b[l],
                                      sgu_w[l], sgu_b[l], gains[2][0], gains[3][0], S, W)
        x1e, route, cnt = _outproj_router_call([mix_a, mix_b, mix_c, mix_d], w_out[l].astype(BF16), xs,
                                               ln1_g[l], ln1_b[l], rw, rb, alpha)
        dest, block_a, block_b, block_active, fill_flag, n_rows = _dispatch_tables(route, cnt, T)
        xd = _dispatch_call(fill_flag, dest, x1e, n_rows)
        y = _experts_call(block_a, block_b, block_active, xd, w1, w3, w2, l)
        xs = _combine_call(dest, y, x1e, ln2_g[l], ln2_b[l], alpha)
    return xs.reshape(B, S, D)
```

```python
import functools
import math

import jax
import jax.numpy as jnp
from jax import lax
from jax.experimental import pallas as pl
from jax.experimental.pallas import tpu as pltpu

N_MIXERS = 4
HEAD_DIM = 64
HEADS = 4
CONV_WIDTH = 3
SGU_BLOCK = 128
CHUNK = 64
N_EXPERT_GROUPS = 4
EXPERTS_PER_GROUP = 8
N_EXPERTS = N_EXPERT_GROUPS * EXPERTS_PER_GROUP
TOP_K = 2
LN_EPS = 1e-5
RMS_EPS = 1e-6

LANES = 128
V7X_VMEM_BYTES = 64 * 1024 * 1024
VMEM_LIMIT = 48 * 1024 * 1024
EXPERT_VMEM_LIMIT = 56 * 1024 * 1024

ROW_TILE = 512
DISPATCH_TILE = 2048
ATT_Q = 256
ATT_K = 256
EXPERT_ROWS = 128
EXPERT_STEP_ROWS = 512
ROUTE_LANES = LANES

F32 = jnp.float32
BF16 = jnp.bfloat16


def _cparams(sem, vmem=None):
    return pltpu.CompilerParams(dimension_semantics=sem, vmem_limit_bytes=vmem)


def _layer_norm_rows(y, g, b):
    mu = jnp.mean(y, axis=-1, keepdims=True)
    d = y - mu
    var = jnp.mean(d * d, axis=-1, keepdims=True)
    return d * lax.rsqrt(var + LN_EPS) * g + b


def _softplus(z):
    return jnp.maximum(z, 0.0) + jnp.log1p(jnp.exp(-jnp.abs(z)))


def _inproj_kernel(x_ref, w_ref, *refs, widths, entry_norm):
    if entry_norm:
        g_ref, b_ref, qa_ref, qb_ref, cv_ref, sg_ref, f_ref, xn_ref = refs
        x = _layer_norm_rows(x_ref[...], g_ref[...], b_ref[...])
        xn_ref[...] = x
    else:
        qa_ref, qb_ref, cv_ref, sg_ref, f_ref = refs
        x = x_ref[...]
    xb = x.astype(BF16)
    off = 0
    for ref, wd in zip((qa_ref, qb_ref, cv_ref, sg_ref, f_ref), widths):
        ref[...] = jnp.dot(xb, w_ref[:, off:off + wd], preferred_element_type=F32).astype(ref.dtype)
        off += wd


def _inproj_call(x, w_all, W, entry_ln=None):
    T, D = x.shape
    widths = (3 * W, 3 * W, 3 * W, 2 * W, LANES)
    NC = sum(widths)
    dts = (BF16, BF16, BF16, BF16, F32)
    row_spec = pl.BlockSpec((ROW_TILE, D), lambda i: (i, 0))
    vec_spec = pl.BlockSpec((1, D), lambda i: (0, 0))
    in_specs = [row_spec, pl.BlockSpec((D, NC), lambda i: (0, 0))]
    out_shape = [jax.ShapeDtypeStruct((T, wd), dt) for wd, dt in zip(widths, dts)]
    out_specs = [pl.BlockSpec((ROW_TILE, wd), lambda i: (i, 0)) for wd in widths]
    args = [x, w_all]
    if entry_ln is not None:
        in_specs += [vec_spec, vec_spec]
        out_shape.append(jax.ShapeDtypeStruct((T, D), F32))
        out_specs.append(row_spec)
        args += [entry_ln[0].reshape(1, D), entry_ln[1].reshape(1, D)]
    return pl.pallas_call(
        functools.partial(_inproj_kernel, widths=widths, entry_norm=entry_ln is not None),
        out_shape=out_shape,
        grid=(T // ROW_TILE,),
        in_specs=in_specs,
        out_specs=out_specs,
        compiler_params=_cparams(("parallel",), VMEM_LIMIT),
        name="inproj",
    )(*args)


def _fcum_kernel(f_ref, b_ref, tri_ref, c_ref):
    R, S = f_ref.shape
    y = f_ref[...] + b_ref[...]
    lf = -_softplus(-y)
    tri = tri_ref[...]
    carry = jnp.zeros((R, 1), F32)
    for blk in range(S // LANES):
        seg = lf[:, blk * LANES:(blk + 1) * LANES]
        s1 = seg.astype(BF16)
        r1 = seg - s1.astype(F32)
        s2 = r1.astype(BF16)
        s3 = (r1 - s2.astype(F32)).astype(BF16)
        cs = (jnp.dot(s1, tri, preferred_element_type=F32)
              + jnp.dot(s2, tri, preferred_element_type=F32)
              + jnp.dot(s3, tri, preferred_element_type=F32)) + carry
        c_ref[:, blk * LANES:(blk + 1) * LANES] = cs
        carry = cs[:, LANES - 1:LANES]


def _fcum_call(f_rows, b_rows):
    R, S = f_rows.shape
    idx = jnp.arange(LANES)
    tri = (idx[:, None] <= idx[None, :]).astype(BF16)
    return pl.pallas_call(
        _fcum_kernel,
        out_shape=jax.ShapeDtypeStruct((R, S), F32),
        name="forget_cumsum",
    )(f_rows, b_rows, tri)


def _rms_group_store(o_ref, out, g_ref):
    ms = jnp.mean(out * out, axis=-1, keepdims=True)
    o_ref[...] = (out * lax.rsqrt(ms + RMS_EPS) * g_ref[...]).astype(o_ref.dtype)


def _head_masks():
    lane = lax.broadcasted_iota(jnp.int32, (1, LANES), 1)
    return [lane < HEAD_DIM, lane >= HEAD_DIM]


def _nt_dot(a, b):
    return lax.dot_general(a, b, (((1,), (1,)), ((), ())), preferred_element_type=F32)


SB_DEAD = -104.0


def _neg_softplus(z):
    return jnp.minimum(-z, 0.0) - jnp.log(1.0 + jnp.exp(-jnp.abs(z)))


def _sb_attn_kernel(q_ref, k_ref, v_ref, m_ref, g_ref, o_ref):
    Q = q_ref.shape[0]
    assert Q == ATT_K
    i = pl.program_id(1)
    row = lax.broadcasted_iota(jnp.int32, (Q, ATT_K), 0)
    col = lax.broadcasted_iota(jnp.int32, (Q, ATT_K), 1)
    diag_mask = col < row
    hmasks = _head_masks()

    def tile(qms, j, state, mask):
        ks = pl.multiple_of(j * ATT_K, ATT_K)
        k_ts = [k_ref[pl.ds(ks, ATT_K), p * LANES:(p + 1) * LANES] for p in range(HEADS // 2)]
        v_ts = [v_ref[pl.ds(ks, ATT_K), p * LANES:(p + 1) * LANES] for p in range(HEADS // 2)]
        later_mat = m_ref[...]
        zs = [_nt_dot(qms[h], k_ts[h // 2]) for h in range(HEADS)]
        mid = []
        for h in range(HEADS):
            log_1m = _neg_softplus(zs[h])
            log_b = zs[h] + log_1m
            if mask is not None:
                log_1m = jnp.where(mask, log_1m, 0.0)
            hi = log_1m.astype(BF16)
            lo = (log_1m - hi.astype(F32)).astype(BF16)
            later = (jnp.dot(hi, later_mat, preferred_element_type=F32)
                     + jnp.dot(lo, later_mat, preferred_element_type=F32)) + state[2 * h]
            mid.append((log_b, later, log_1m[:, 0:1]))
        out = []
        for h in range(HEADS):
            log_b, later, first = mid[h]
            w = jnp.exp(log_b + later)
            if mask is not None:
                w = jnp.where(mask, w, 0.0)
            acc = state[2 * h + 1] + jnp.dot(w.astype(BF16), v_ts[h // 2], preferred_element_type=F32)
            out.extend((later[:, 0:1] + first, acc))
        return tuple(out)

    qms = []
    for h in range(HEADS):
        q_pair = q_ref[:, (h // 2) * LANES:(h // 2 + 1) * LANES]
        qms.append(jnp.where(hmasks[h % 2], q_pair, jnp.zeros_like(q_pair)))
    def alive(state):
        top = state[0]
        for h in range(1, HEADS):
            top = jnp.maximum(top, state[2 * h])
        return (jnp.max(top) >= SB_DEAD).astype(jnp.int32)

    state = (jnp.zeros((Q, 1), F32), jnp.zeros((Q, LANES), F32)) * HEADS
    state = tile(qms, i, state, diag_mask)

    def cond(c):
        return (c[0] <= i) & (c[1] > 0)

    def body(c):
        st = tile(qms, i - c[0], tuple(c[2:]), None)
        return (c[0] + 1, alive(st)) + st

    state = lax.while_loop(cond, body, (jnp.int32(1), alive(state)) + state)[2:]
    outs = [jnp.where(hmasks[0], state[4 * p + 1], state[4 * p + 3]) for p in range(HEADS // 2)]
    _rms_group_store(o_ref, jnp.concatenate(outs, axis=-1), g_ref)


def _fox_attn_kernel(q_ref, k_ref, v_ref, c_ref, g_ref, o_ref):
    Q = q_ref.shape[0]
    assert Q == ATT_K
    i = pl.program_id(1)
    nk = c_ref.shape[1] // HEADS
    row = lax.broadcasted_iota(jnp.int32, (Q, ATT_K), 0)
    col = lax.broadcasted_iota(jnp.int32, (Q, ATT_K), 1)
    diag_mask = col <= row
    hmasks = _head_masks()
    neg = jnp.finfo(F32).min

    lane = lax.broadcasted_iota(jnp.int32, (1, LANES), 1)
    ones_lane = [LANES - 1, 0]
    one = jnp.ones((), BF16)

    def tile(qms, j, state, mask):
        ks = pl.multiple_of(j * ATT_K, ATT_K)
        k_ts = [k_ref[pl.ds(ks, ATT_K), p * LANES:(p + 1) * LANES] for p in range(HEADS // 2)]
        v_ts = [v_ref[pl.ds(ks, ATT_K), p * LANES:(p + 1) * LANES] for p in range(HEADS // 2)]
        ss = []
        for h in range(HEADS):
            c_row = c_ref[0, pl.ds(h * nk + j, 1), :]
            s = _nt_dot(qms[h], k_ts[h // 2]) - c_row
            if mask is not None:
                s = jnp.where(mask, s, neg)
            ss.append(s)
        mid = []
        for h in range(HEADS):
            m = state[2 * h]
            m_new = jnp.maximum(m, jnp.max(ss[h], axis=-1, keepdims=True))
            alpha = jnp.exp(m - m_new)
            p = jnp.exp(ss[h] - m_new)
            mid.append((m_new, alpha, p.astype(BF16)))
        out = []
        for h in range(HEADS):
            m_new, alpha, p = mid[h]
            v_aug = jnp.where(lane == ones_lane[h % 2], one, v_ts[h // 2])
            acc = alpha * state[2 * h + 1] + jnp.dot(p, v_aug, preferred_element_type=F32)
            out.extend((m_new, acc))
        return tuple(out)

    qms = []
    for h in range(HEADS):
        q_pair = q_ref[:, (h // 2) * LANES:(h // 2 + 1) * LANES]
        qms.append(jnp.where(hmasks[h % 2], q_pair, jnp.zeros_like(q_pair)))
    state = (jnp.full((Q, 1), neg, F32), jnp.zeros((Q, LANES), F32)) * HEADS
    state = tile(qms, i, state, diag_mask)
    state = lax.fori_loop(1, i + 1, lambda jj, st: tile(qms, i - jj, st, None), state)
    heads = []
    for h in range(HEADS):
        acc = state[2 * h + 1]
        denom = acc[:, ones_lane[h % 2]:ones_lane[h % 2] + 1]
        heads.append(acc / denom)
    outs = [jnp.where(hmasks[0], heads[2 * p], heads[2 * p + 1]) for p in range(HEADS // 2)]
    _rms_group_store(o_ref, jnp.concatenate(outs, axis=-1), g_ref)


def _attn_specs(B, S, W):
    nq = S // ATT_Q
    q_spec = pl.BlockSpec((ATT_Q, W), lambda b, i: (b * nq + i, 0))
    k_spec = pl.BlockSpec((S, W), lambda b, i: (b, 1))
    v_spec = pl.BlockSpec((S, W), lambda b, i: (b, 2))
    g_spec = pl.BlockSpec((1, W), lambda b, i: (0, 0))
    o_spec = pl.BlockSpec((ATT_Q, W), lambda b, i: (b * nq + i, 0))
    return nq, q_spec, k_spec, v_spec, g_spec, o_spec


def _sb_attn_call(qkv, gain, B, S, W):
    T = B * S
    nq, q_spec, k_spec, v_spec, g_spec, o_spec = _attn_specs(B, S, W)
    idx = jnp.arange(ATT_K)
    later_mat = (idx[:, None] > idx[None, :]).astype(BF16)
    return pl.pallas_call(
        _sb_attn_kernel,
        out_shape=jax.ShapeDtypeStruct((T, W), BF16),
        grid=(B, nq),
        in_specs=[q_spec, k_spec, v_spec,
                  pl.BlockSpec((ATT_K, ATT_K), lambda b, i: (0, 0)), g_spec],
        out_specs=o_spec,
        compiler_params=_cparams(("parallel", "arbitrary"), VMEM_LIMIT),
        name="sb_attn",
    )(qkv, qkv, qkv, later_mat, gain)


def _fox_attn_call(qkv, c_tiles, gain, B, S, W):
    T = B * S
    nq, q_spec, k_spec, v_spec, g_spec, o_spec = _attn_specs(B, S, W)
    rows = c_tiles.shape[1]
    return pl.pallas_call(
        _fox_attn_kernel,
        out_shape=jax.ShapeDtypeStruct((T, W), BF16),
        grid=(B, nq),
        in_specs=[q_spec, k_spec, v_spec,
                  pl.BlockSpec((1, rows, ATT_K), lambda b, i: (b, 0, 0)), g_spec],
        out_specs=o_spec,
        compiler_params=_cparams(("parallel", "arbitrary"), VMEM_LIMIT),
        name="fox_attn",
    )(qkv, qkv, qkv, c_tiles, gain)


HALO = 16


def _gelu_tanh(x):
    c = math.sqrt(2.0 / math.pi)
    return 0.5 * x * (1.0 + jnp.tanh(c * (x + 0.044715 * (x * x * x))))


def _conv_sgu_kernel(cv_ref, halo_ref, sg_ref, cw_ref, cb_ref, lg_ref, lb_ref, ws_ref, bs_ref,
                     gc_ref, gd_ref, oc_ref, od_ref, *, tiles_per_seq):
    R, W3 = cv_ref.shape
    W = W3 // 3
    i = pl.program_id(0)
    z = cv_ref[:, 2 * W:3 * W].astype(F32) * cv_ref[:, 0:W].astype(F32)
    zh = halo_ref[:, 2 * W:3 * W].astype(F32) * halo_ref[:, 0:W].astype(F32)
    zh = jnp.where(i % tiles_per_seq == 0, jnp.zeros_like(zh), zh)
    zf = jnp.concatenate([zh, z], axis=0)
    z1 = pltpu.roll(zf, 1, 0)[HALO:]
    z2 = pltpu.roll(zf, 2, 0)[HALO:]
    y = cb_ref[...] + cw_ref[0:1, :] * z2
    y = y + cw_ref[1:2, :] * z1
    y = y + cw_ref[2:3, :] * z
    _rms_group_store(oc_ref, cv_ref[:, W:2 * W].astype(F32) * y, gc_ref)
    gel = _gelu_tanh(sg_ref[...].astype(F32))
    u = gel[:, 0:W]
    vn = _layer_norm_rows(gel[:, W:2 * W], lg_ref[...], lb_ref[...]).astype(BF16)
    lane = lax.broadcasted_iota(jnp.int32, (1, W), 1)
    pi = lax.broadcasted_iota(jnp.int32, (SGU_BLOCK, SGU_BLOCK), 0) // CHUNK
    pj = lax.broadcasted_iota(jnp.int32, (SGU_BLOCK, SGU_BLOCK), 1) // CHUNK
    w_m = [jnp.where(pj <= pi, ws_ref[g], 0.0).astype(BF16) for g in range(HEADS)]
    blocks = []
    for n in range(R // SGU_BLOCK):
        vb = vn[n * SGU_BLOCK:(n + 1) * SGU_BLOCK, :]
        mixed = jnp.dot(w_m[0], vb, preferred_element_type=F32)
        for g in range(1, HEADS):
            mg = jnp.dot(w_m[g], vb, preferred_element_type=F32)
            mixed = jnp.where(lane >= g * HEAD_DIM, mg, mixed)
        blocks.append(mixed + bs_ref[...])
    mixed = jnp.concatenate(blocks, axis=0)
    _rms_group_store(od_ref, u * mixed, gd_ref)


def _conv_sgu_call(cv, sg, conv_w, conv_b, ln_g, ln_b, sgu_w, sgu_b, g_c, g_d, S, W):
    T = cv.shape[0]
    R = ROW_TILE
    tiles_per_seq = S // R
    bias_tile = jnp.repeat(sgu_b.T, HEAD_DIM, axis=1)
    row = lambda a: a.reshape(1, W)
    const = lambda shape: pl.BlockSpec(shape, lambda i: tuple(0 for _ in shape))
    return pl.pallas_call(
        functools.partial(_conv_sgu_kernel, tiles_per_seq=tiles_per_seq),
        out_shape=[jax.ShapeDtypeStruct((T, W), BF16)] * 2,
        grid=(T // R,),
        in_specs=[pl.BlockSpec((R, 3 * W), lambda i: (i, 0)),
                  pl.BlockSpec((HALO, 3 * W), lambda i: (jnp.maximum(i * (R // HALO) - 1, 0), 0)),
                  pl.BlockSpec((R, 2 * W), lambda i: (i, 0)),
                  const((CONV_WIDTH, W)), const((1, W)), const((1, W)), const((1, W)),
                  const((HEADS, SGU_BLOCK, SGU_BLOCK)), const((SGU_BLOCK, W)),
                  const((1, W)), const((1, W))],
        out_specs=[pl.BlockSpec((R, W), lambda i: (i, 0))] * 2,
        compiler_params=_cparams(("parallel",), VMEM_LIMIT),
        name="conv_sgu",
    )(cv, cv, sg, conv_w, row(conv_b), row(ln_g), row(ln_b), sgu_w, bias_tile, row(g_c), row(g_d))


ROUTE_CHUNK = ROW_TILE
PAIRS_PER_GROUP = EXPERTS_PER_GROUP * (EXPERTS_PER_GROUP - 1) // 2
N_PAIRS = N_EXPERT_GROUPS * PAIRS_PER_GROUP


def _outproj_router_kernel(ma_ref, mb_ref, mc_ref, md_ref, wo_ref, x_ref, g_ref, b_ref,
                           rwa_ref, rwh_ref, rb_ref, tri_ref, x1_ref, route_ref, cnt_ref, base_ref, *, alpha):
    @pl.when(pl.program_id(0) == 0)
    def _():
        base_ref[...] = jnp.zeros(base_ref.shape, base_ref.dtype)

    base = base_ref[...]
    for c in range(x_ref.shape[0] // ROUTE_CHUNK):
        rows = slice(c * ROUTE_CHUNK, (c + 1) * ROUTE_CHUNK)
        base = _outproj_route_rows(rows, ma_ref, mb_ref, mc_ref, md_ref, wo_ref, x_ref, g_ref, b_ref,
                                   rwa_ref, rwh_ref, rb_ref, tri_ref, x1_ref, route_ref, base, alpha)
    base_ref[...] = base
    cnt_ref[...] = jnp.broadcast_to(base, cnt_ref.shape)


def _outproj_route_rows(rows, ma_ref, mb_ref, mc_ref, md_ref, wo_ref, x_ref, g_ref, b_ref,
                        rwa_ref, rwh_ref, rb_ref, tri_ref, x1_ref, route_ref, base, alpha):
    mix = jnp.concatenate([ref[rows, :] for ref in (ma_ref, mb_ref, mc_ref, md_ref)], axis=1)
    acc = jnp.dot(mix, wo_ref[...], preferred_element_type=F32)
    x1 = _layer_norm_rows(alpha * x_ref[rows, :] + acc, g_ref[...], b_ref[...])
    D = x1.shape[1]
    x1_ref[rows, 0:D] = x1
    xh = x1.astype(BF16)
    xl = (x1 - xh.astype(F32)).astype(BF16)
    both = jnp.dot(xh, rwa_ref[...], preferred_element_type=F32)
    logits = (both[:, 0:ROUTE_LANES] + both[:, ROUTE_LANES:2 * ROUTE_LANES]
              + jnp.dot(xl, rwh_ref[...], preferred_element_type=F32)) + rb_ref[...]
    R = logits.shape[0]
    lane = lax.broadcasted_iota(jnp.int32, (R, ROUTE_LANES), 1).astype(F32)
    big = float(ROUTE_LANES)
    ninf = jnp.finfo(F32).min
    gmask = lane < N_EXPERT_GROUPS
    lg = jnp.where(gmask, logits, ninf)
    mg = jnp.max(lg, axis=-1, keepdims=True)
    sum_g = jnp.sum(jnp.where(gmask, jnp.exp(lg - mg), 0.0), axis=-1, keepdims=True)
    p_gsel = 1.0 / sum_g
    g_sel = jnp.min(jnp.where(lg == mg, lane, big), axis=-1, keepdims=True)
    e_lo = N_EXPERT_GROUPS + EXPERTS_PER_GROUP * g_sel
    emask = (lane >= e_lo) & (lane < e_lo + EXPERTS_PER_GROUP)
    le = jnp.where(emask, logits, ninf)
    me = jnp.max(le, axis=-1, keepdims=True)
    ee = jnp.where(emask, jnp.exp(le - me), 0.0)
    pe = ee / jnp.sum(ee, axis=-1, keepdims=True)
    pe = jnp.where(emask, pe, -1.0)
    p1 = jnp.max(pe, axis=-1, keepdims=True)
    i1 = jnp.min(jnp.where(pe == p1, lane, big), axis=-1, keepdims=True)
    pe2 = jnp.where(lane == i1, -1.0, pe)
    p2 = jnp.max(pe2, axis=-1, keepdims=True)
    i2 = jnp.min(jnp.where(pe2 == p2, lane, big), axis=-1, keepdims=True)
    denom = p1 + p2
    gate1 = p_gsel * p1 / denom
    gate2 = p_gsel * p2 / denom
    a1 = i1 - e_lo
    a2 = i2 - e_lo
    lo = jnp.minimum(a1, a2)
    hi = jnp.maximum(a1, a2)
    first_low = a1 < a2
    gate_lo = jnp.where(first_low, gate1, gate2)
    gate_hi = jnp.where(first_low, gate2, gate1)
    pair = g_sel * PAIRS_PER_GROUP + lo * ((2 * EXPERTS_PER_GROUP - 1) - lo) * 0.5 + (hi - lo - 1.0)
    ex_lo = g_sel * EXPERTS_PER_GROUP + lo
    ex_hi = g_sel * EXPERTS_PER_GROUP + hi
    sel = lane == pair
    one = jnp.where(sel, 1.0, 0.0)
    before = jnp.dot(tri_ref[...], one.astype(BF16), preferred_element_type=F32) + base
    rank = jnp.sum(jnp.where(sel, before, 0.0), axis=-1, keepdims=True)
    route = jnp.where(lane == 0, ex_lo, jnp.where(lane == 1, ex_hi, jnp.where(lane == 2, gate_lo,
            jnp.where(lane == 3, gate_hi, jnp.where(lane == 4, rank, pair)))))
    route_ref[rows, :] = route
    x1_ref[rows, D:D + ROUTE_LANES] = route
    return before[R - 1:R, :] + one[R - 1:R, :]


def _outproj_router_call(mixes, w_out, x, g, b, rw, rb, alpha):
    T, D = x.shape
    W = mixes[0].shape[1]
    R = ROUTE_CHUNK
    rw_hi = rw.astype(BF16)
    rw_lo = (rw - rw_hi.astype(F32)).astype(BF16)
    rwa = jnp.concatenate([rw_hi, rw_lo], axis=1)
    idx = jnp.arange(ROUTE_CHUNK)
    tri = (idx[None, :] < idx[:, None]).astype(BF16)
    const = lambda shape: pl.BlockSpec(shape, lambda i: tuple(0 for _ in shape))
    return pl.pallas_call(
        functools.partial(_outproj_router_kernel, alpha=alpha),
        out_shape=[jax.ShapeDtypeStruct((T, D + ROUTE_LANES), F32), jax.ShapeDtypeStruct((T, ROUTE_LANES), F32),
                   jax.ShapeDtypeStruct((8, ROUTE_LANES), F32)],
        grid=(T // R,),
        in_specs=[pl.BlockSpec((R, W), lambda i: (i, 0))] * 4
        + [const((D, D)), pl.BlockSpec((R, D), lambda i: (i, 0)), const((1, D)), const((1, D)),
           const((D, 2 * ROUTE_LANES)), const((D, ROUTE_LANES)), const((1, ROUTE_LANES)),
           const((ROUTE_CHUNK, ROUTE_CHUNK))],
        out_specs=[pl.BlockSpec((R, D + ROUTE_LANES), lambda i: (i, 0)),
                   pl.BlockSpec((R, ROUTE_LANES), lambda i: (i, 0)),
                   const((8, ROUTE_LANES))],
        scratch_shapes=[pltpu.VMEM((1, ROUTE_LANES), F32)],
        compiler_params=_cparams(("arbitrary",), VMEM_LIMIT),
        name="outproj_router",
    )(*mixes, w_out, x, g.reshape(1, D), b.reshape(1, D), rwa, rw_hi, rb, tri)


IDX_SLOTS = 3
SUBLANES = 8


def _dispatch_kernel(fill_ref, dest_hbm, x_ref, xd_hbm, dsm, zbuf, sem_idx, sem_row, sem_fill):
    i = pl.program_id(0)
    n = pl.num_programs(0)
    R = x_ref.shape[0] * SUBLANES
    TB = zbuf.shape[0]
    NI = R

    def idx_copy(t):
        s = t % 2
        return pltpu.make_async_copy(dest_hbm.at[t], dsm.at[pl.ds(s * NI, NI)], sem_idx.at[s])

    def fill_copy(blk):
        return pltpu.make_async_copy(zbuf, xd_hbm.at[pl.ds(blk * TB, TB)], sem_fill)

    @pl.when(i == 0)
    def _():
        idx_copy(0).start()
        zbuf[...] = jnp.zeros(zbuf.shape, zbuf.dtype)

        def start_body(blk, _):
            @pl.when(fill_ref[blk] != 0)
            def _():
                fill_copy(blk).start()
            return 0

        def wait_body(blk, _):
            @pl.when(fill_ref[blk] != 0)
            def _():
                fill_copy(blk).wait()
            return 0

        lax.fori_loop(0, fill_ref.shape[0], start_body, 0)
        lax.fori_loop(0, fill_ref.shape[0], wait_body, 0)

    idx_copy(i).wait()

    @pl.when(i + 1 < n)
    def _():
        idx_copy(i + 1).start()

    base = (i % 2) * NI

    def body(grp, _):
        at = base + SUBLANES * grp
        for u in range(SUBLANES):
            d = dsm[at + u]
            pltpu.make_async_copy(
                x_ref.at[grp, pl.ds(u, 1), :],
                xd_hbm.at[lax.shift_right_logical(d, 3), pl.ds(d & (SUBLANES - 1), 1), :],
                sem_row).start(priority=u % 2)
        return 0

    lax.fori_loop(0, R // SUBLANES, body, 0)
    pltpu.make_async_copy(x_ref, x_ref, sem_row).wait()


def _dispatch_call(fill_flag, dest, x1, n_rows):
    T, D = x1.shape
    R = DISPATCH_TILE
    dest = dest.reshape(T // R, R)
    grid_spec = pltpu.PrefetchScalarGridSpec(
        num_scalar_prefetch=1,
        grid=(T // R,),
        in_specs=[pl.BlockSpec(memory_space=pl.ANY),
                  pl.BlockSpec((R // SUBLANES, SUBLANES, D), lambda i, ff: (i, 0, 0))],
        out_specs=pl.BlockSpec(memory_space=pl.ANY),
        scratch_shapes=[pltpu.SMEM((2 * R,), jnp.int32),
                        pltpu.VMEM((EXPERT_ROWS // SUBLANES, SUBLANES, D), F32),
                        pltpu.SemaphoreType.DMA((2,)),
                        pltpu.SemaphoreType.DMA,
                        pltpu.SemaphoreType.DMA],
    )
    xd = pl.pallas_call(
        _dispatch_kernel,
        out_shape=jax.ShapeDtypeStruct((n_rows // SUBLANES, SUBLANES, D), F32),
        grid_spec=grid_spec,
        compiler_params=_cparams(("arbitrary",), VMEM_LIMIT),
        name="moe_dispatch",
    )(fill_flag, dest, x1.reshape(T // SUBLANES, SUBLANES, D))
    return xd.reshape(n_rows, D)


def _experts_kernel(ea_ref, eb_ref, act_ref, x_ref, w1_ref, w3_ref, w2_ref, y_ref, wb1, wb3, wb2):
    i = pl.program_id(0)
    D = y_ref.shape[1]
    RB = EXPERT_ROWS
    per_step = x_ref.shape[0] // RB
    first = i * per_step
    group = ea_ref[first] // EXPERTS_PER_GROUP
    prev = ea_ref[jnp.maximum(first, 1) - 1] // EXPERTS_PER_GROUP

    @pl.when((i == 0) | (group != prev))
    def _():
        for e in range(EXPERTS_PER_GROUP):
            wb1[e] = w1_ref[0, e].astype(BF16)
            wb3[e] = w3_ref[0, e].astype(BF16)
            wb2[e] = w2_ref[0, e].astype(BF16)

    def expert(xb, e):
        h1 = jnp.dot(xb, wb1[e], preferred_element_type=F32)
        h3 = jnp.dot(xb, wb3[e], preferred_element_type=F32)
        h = (h1 * jax.nn.sigmoid(h1)) * h3
        return jnp.dot(h.astype(BF16), wb2[e], preferred_element_type=F32)

    for s in range(per_step):
        blk = first + s
        rows = slice(s * RB, (s + 1) * RB)

        @pl.when(act_ref[blk] != 0)
        def _(blk=blk, rows=rows):
            xb = x_ref[rows, 0:D].astype(BF16)
            ya = expert(xb, ea_ref[blk] % EXPERTS_PER_GROUP) * x_ref[rows, D + 2:D + 3]
            yb = expert(xb, eb_ref[blk] % EXPERTS_PER_GROUP) * x_ref[rows, D + 3:D + 4]
            y_ref[rows, :] = ya + yb

        @pl.when(act_ref[blk] == 0)
        def _(rows=rows):
            y_ref[rows, :] = jnp.zeros((RB, D), y_ref.dtype)


def _experts_call(block_a, block_b, block_active, xd, w1, w3, w2, layer):
    P, DX = xd.shape
    D = DX - ROUTE_LANES
    SR = EXPERT_STEP_ROWS
    per_step = SR // EXPERT_ROWS
    DE = w1.shape[3]
    EG = EXPERTS_PER_GROUP
    by_group = lambda w: w.reshape(w.shape[0] * N_EXPERT_GROUPS, EG, w.shape[2], w.shape[3])
    wmap = lambda i, ea, eb, act: (layer * N_EXPERT_GROUPS + ea[i * per_step] // EG, 0, 0, 0)
    once = pl.Buffered(1)
    grid_spec = pltpu.PrefetchScalarGridSpec(
        num_scalar_prefetch=3,
        grid=(P // SR,),
        in_specs=[pl.BlockSpec((SR, DX), lambda i, ea, eb, act: (i, 0)),
                  pl.BlockSpec((1, EG, D, DE), wmap, pipeline_mode=once),
                  pl.BlockSpec((1, EG, D, DE), wmap, pipeline_mode=once),
                  pl.BlockSpec((1, EG, DE, D), wmap, pipeline_mode=once)],
        out_specs=pl.BlockSpec((SR, D), lambda i, ea, eb, act: (i, 0)),
        scratch_shapes=[pltpu.VMEM((EG, D, DE), BF16), pltpu.VMEM((EG, D, DE), BF16),
                        pltpu.VMEM((EG, DE, D), BF16)],
    )
    return pl.pallas_call(
        _experts_kernel,
        out_shape=jax.ShapeDtypeStruct((P, D), F32),
        grid_spec=grid_spec,
        compiler_params=_cparams(("arbitrary",), EXPERT_VMEM_LIMIT),
        name="experts",
    )(block_a, block_b, block_active, xd, by_group(w1), by_group(w3), by_group(w2))


def _combine_kernel(dest_hbm, y_hbm, x_ref, g_ref, b_ref, o_ref, dsm, ybuf, sem_idx, sem_row, *, alpha):
    i = pl.program_id(0)
    n = pl.num_programs(0)
    G = x_ref.shape[0]
    D = o_ref.shape[2]
    NI = G * SUBLANES

    def idx_copy(t):
        s = t % IDX_SLOTS
        return pltpu.make_async_copy(dest_hbm.at[t], dsm.at[pl.ds(s * NI, NI)], sem_idx.at[s])

    def issue_gather(t):
        base = (t % IDX_SLOTS) * NI
        s2 = t % 2

        def body(grp, _):
            at = base + SUBLANES * grp
            for u in range(SUBLANES):
                d = dsm[at + u]
                pltpu.make_async_copy(
                    y_hbm.at[lax.shift_right_logical(d, 3), pl.ds(d & (SUBLANES - 1), 1), :],
                    ybuf.at[s2, grp, pl.ds(u, 1), :], sem_row.at[s2]).start(priority=u % 2)
            return 0

        lax.fori_loop(0, G, body, 0)

    @pl.when(i == 0)
    def _():
        c0 = idx_copy(0)
        c0.start()
        c0.wait()
        issue_gather(0)

        @pl.when(n > 1)
        def _():
            idx_copy(1).start()

    @pl.when(i + 1 < n)
    def _():
        idx_copy(i + 1).wait()
        issue_gather(i + 1)

    @pl.when(i + 2 < n)
    def _():
        idx_copy(i + 2).start()

    s2 = i % 2
    pltpu.make_async_copy(ybuf.at[s2], ybuf.at[s2], sem_row.at[s2]).wait()
    o_ref[...] = _layer_norm_rows(alpha * x_ref[:, :, 0:D] + ybuf[s2], g_ref[...], b_ref[...])


def _combine_call(dest, y, x1e, g, b, alpha):
    T = x1e.shape[0]
    D = y.shape[1]
    R = ROW_TILE
    dest = dest.reshape(T // R, R)
    G = R // SUBLANES
    tiles = lambda a: a.reshape(a.shape[0] // SUBLANES, SUBLANES, a.shape[1])
    rows_spec = lambda lanes: pl.BlockSpec((G, SUBLANES, lanes), lambda i: (i, 0, 0))
    vec_spec = pl.BlockSpec((1, 1, D), lambda i: (0, 0, 0))
    out = pl.pallas_call(
        functools.partial(_combine_kernel, alpha=alpha),
        out_shape=jax.ShapeDtypeStruct((T // SUBLANES, SUBLANES, D), F32),
        grid=(T // R,),
        in_specs=[pl.BlockSpec(memory_space=pl.ANY), pl.BlockSpec(memory_space=pl.ANY),
                  rows_spec(x1e.shape[1]), vec_spec, vec_spec],
        out_specs=rows_spec(D),
        scratch_shapes=[pltpu.SMEM((IDX_SLOTS * R,), jnp.int32),
                        pltpu.VMEM((2, G, SUBLANES, D), F32),
                        pltpu.SemaphoreType.DMA((IDX_SLOTS,)),
                        pltpu.SemaphoreType.DMA((2,))],
        compiler_params=_cparams(("arbitrary",), VMEM_LIMIT),
        name="combine_ln",
    )(dest, tiles(y), tiles(x1e), g.reshape(1, 1, D), b.reshape(1, 1, D))
    return out.reshape(T, D)


def _pair_expert_tables():
    lo_hi = [(lo, hi) for lo in range(EXPERTS_PER_GROUP) for hi in range(lo + 1, EXPERTS_PER_GROUP)]
    ea = [g * EXPERTS_PER_GROUP + lo for g in range(N_EXPERT_GROUPS) for lo, _ in lo_hi]
    eb = [g * EXPERTS_PER_GROUP + hi for g in range(N_EXPERT_GROUPS) for _, hi in lo_hi]
    return jnp.asarray(ea, jnp.int32), jnp.asarray(eb, jnp.int32)


def _dispatch_tables(route, cnt, T):
    RB = EXPERT_ROWS
    SR = EXPERT_STEP_ROWS
    NG, PG = N_EXPERT_GROUPS, PAIRS_PER_GROUP
    pairs = jnp.arange(N_PAIRS, dtype=jnp.int32)
    counts = cnt[0, 0:N_PAIRS].astype(jnp.int32)
    padded = ((counts + RB - 1) // RB * RB).reshape(NG, PG)
    in_group_end = jnp.cumsum(padded, axis=1)
    group_rows = (in_group_end[:, -1] + SR - 1) // SR * SR
    group_end = jnp.cumsum(group_rows)
    group_start = group_end - group_rows
    p_starts = (group_start[:, None] + in_group_end - padded).reshape(N_PAIRS)
    p_ends = p_starts + padded.reshape(N_PAIRS)
    rank = route[:, 4].astype(jnp.int32)
    pair = route[:, 5].astype(jnp.int32)
    start_of = jnp.sum(jnp.where(pair[:, None] == pairs[None, :], p_starts[None, :], 0), axis=-1)
    dest = start_of + rank
    P = -(-(T + N_PAIRS * RB + NG * SR) // SR) * SR
    n_blocks = P // RB
    row0 = jnp.arange(n_blocks, dtype=jnp.int32) * RB
    inside = (row0[:, None] >= p_starts[None, :]) & (row0[:, None] < p_ends[None, :])
    ea, eb = _pair_expert_tables()
    block_group = jnp.minimum(jnp.sum((group_end[None, :] <= row0[:, None]).astype(jnp.int32), axis=1), NG - 1)
    block_active = jnp.any(inside, axis=1)
    block_a = jnp.where(block_active, jnp.sum(jnp.where(inside, ea[None, :], 0), axis=1),
                        block_group * EXPERTS_PER_GROUP)
    block_b = jnp.where(block_active, jnp.sum(jnp.where(inside, eb[None, :], 0), axis=1),
                        block_group * EXPERTS_PER_GROUP)
    last_partial = jnp.any((row0[:, None] == (p_ends - RB)[None, :]) & ((counts % RB) != 0)[None, :], axis=1)
    fill_flag = (last_partial | ~block_active).astype(jnp.int32)
    return dest, block_a, block_b, block_active.astype(jnp.int32), fill_flag, P


def kernel(x, ln_in_g, ln_in_b, w_in, b_f, conv_w, conv_b, sgu_ln_g, sgu_ln_b, sgu_w, sgu_b, grp_g, w_out, ln1_g, ln1_b, router_g_w, router_g_b, router_e_w, router_e_b, w1, w3, w2, ln2_g, ln2_b):
    B, S, D = x.shape
    depth = w_in.shape[0]
    W = D // N_MIXERS
    T = B * S
    alpha = (2.0 * depth) ** 0.25
    scale = 1.0 / math.sqrt(HEAD_DIM)

    xs = x.reshape(T, D)
    for l in range(depth):
        wl = w_in[l]
        o_f = 6 * W
        o_cv = o_f + HEADS
        o_sg = o_cv + 3 * W
        f_cols = jnp.pad(wl[:, o_f:o_cv], ((0, 0), (0, LANES - HEADS)))
        w_all = jnp.concatenate(
            [wl[:, 0:W] * scale, wl[:, W:3 * W], wl[:, 3 * W:4 * W] * scale, wl[:, 4 * W:6 * W],
             wl[:, o_cv:o_sg], wl[:, o_sg:], f_cols], axis=1).astype(BF16)
        rw = jnp.pad(jnp.concatenate([router_g_w[l], router_e_w[l]], axis=1),
                     ((0, 0), (0, ROUTE_LANES - N_EXPERT_GROUPS - N_EXPERTS)))
        rb = jnp.pad(jnp.concatenate([router_g_b[l], router_e_b[l]]),
                     (0, ROUTE_LANES - N_EXPERT_GROUPS - N_EXPERTS)).reshape(1, ROUTE_LANES)
        gains = [grp_g[l, k * W:(k + 1) * W].reshape(1, W) for k in range(N_MIXERS)]

        if l == 0:
            qkv_a, qkv_b, cv, sg, f_pad, xs = _inproj_call(xs, w_all, W, entry_ln=(ln_in_g, ln_in_b))
        else:
            qkv_a, qkv_b, cv, sg, f_pad = _inproj_call(xs, w_all, W)
        f_rows = f_pad[:, 0:HEADS].reshape(B, S, HEADS).transpose(0, 2, 1).reshape(B * HEADS, S)
        b_rows = jnp.tile(b_f[l], B).reshape(B * HEADS, 1)
        c = _fcum_call(f_rows, b_rows)
        c_tiles = c.reshape(B, HEADS * (S // ATT_K), ATT_K)
        mix_a = _sb_attn_call(qkv_a, gains[0], B, S, W)
        mix_b = _fox_attn_call(qkv_b, c_tiles, gains[1], B, S, W)
        mix_c, mix_d = _conv_sgu_call(cv, sg, conv_w[l], conv_b[l], sgu_ln_g[l], sgu_ln_b[l],
                                      sgu_w[l], sgu_b[l], gains[2][0], gains[3][0], S, W)
        x1e, route, cnt = _outproj_router_call([mix_a, mix_b, mix_c, mix_d], w_out[l].astype(BF16), xs,
                                               ln1_g[l], ln1_b[l], rw, rb, alpha)
        dest, block_a, block_b, block_active, fill_flag, n_rows = _dispatch_tables(route, cnt, T)
        xd = _dispatch_call(fill_flag, dest, x1e, n_rows)
        y = _experts_call(block_a, block_b, block_active, xd, w1, w3, w2, l)
        xs = _combine_call(dest, y, x1e, ln2_g[l], ln2_b[l], alpha)
    return xs.reshape(B, S, D)
```
